```python
import math
import jax, jax.numpy as jnp
from jax import lax
import numpy as np

D_MODEL = 2048
BATCH = 8
SEQ = 2048
DEPTH = 1

MIX_WIDTH = D_MODEL
ATTN_WIDTH = D_MODEL // 2
CONV_WIDTH = MIX_WIDTH - ATTN_WIDTH
DIFF_HEAD_DIM = 64
N_DIFF_HEADS = ATTN_WIDTH // (2 * DIFF_HEAD_DIM)
DIFF_V_DIM = 2 * DIFF_HEAD_DIM
Q_BLOCK = 128
CONV_K = 3
IN_COLS = 3 * ATTN_WIDTH + 3 * CONV_WIDTH
PEER_HEADS = 8
PEER_N_KEYS = 128
PEER_N_EXPERTS = PEER_N_KEYS * PEER_N_KEYS
PEER_QUERY_DIM = 256
PEER_HALF = PEER_QUERY_DIM // 2
PEER_TOPK = 16
TOKEN_BLOCK = 128
EPS = 1e-6

kernel_name = "hymba_diffattn_shortconv_peer"


def _rms(x, g):
    xf = x.astype(jnp.float32)
    y = xf * lax.rsqrt(jnp.mean(xf * xf, axis=-1, keepdims=True) + EPS)
    return (y * g.astype(jnp.float32)).astype(x.dtype)


def _alibi_slopes(n):
    return jnp.exp2(-8.0 * jnp.arange(1, n + 1, dtype=jnp.float32) / n)


def _diff_attention(q, k, v, lam):
    b, s, h, _, d = q.shape
    nb = s // Q_BLOCK
    scale = 1.0 / math.sqrt(d)
    slopes = _alibi_slopes(h)
    key_pos = jnp.arange(s, dtype=jnp.int32)
    q_blocks = jnp.moveaxis(q.reshape(b, nb, Q_BLOCK, h, 2, d), 1, 0)
    starts = jnp.arange(nb, dtype=jnp.int32) * Q_BLOCK

    def one_block(args):
        qb, start = args
        scores = jnp.einsum("bqhcd,bkhcd->bhcqk", qb, k).astype(jnp.float32) * scale
        q_pos = start + jnp.arange(Q_BLOCK, dtype=jnp.int32)
        dist = (q_pos[:, None] - key_pos[None, :]).astype(jnp.float32)
        scores = scores - slopes[:, None, None, None] * dist
        scores = jnp.where(dist >= 0, scores, -jnp.inf)
        p = jax.nn.softmax(scores, axis=-1)
        a = p[:, :, 0] - lam * p[:, :, 1]
        return jnp.einsum("bhqk,bkhe->bqhe", a.astype(v.dtype), v)

    out = lax.map(one_block, (q_blocks, starts))
    return jnp.moveaxis(out, 0, 1).reshape(b, s, h, v.shape[-1])


def _short_conv(bg, cg, xc, w):
    s = xc.shape[1]
    u = cg * xc
    up = jnp.pad(u, ((0, 0), (CONV_K - 1, 0), (0, 0)))
    y = up[:, 0:s] * w[0]
    for j in range(1, CONV_K):
        y = y + up[:, j:j + s] * w[j]
    return bg * y


def _peer(h, w_q, subkeys, u, v):
    b, s, dm = h.shape
    t = b * s
    ht = h.reshape(t, dm)
    q = (ht @ w_q).reshape(t, PEER_HEADS, 2, PEER_HALF)
    sc = jnp.einsum("thcd,hckd->thck", q, subkeys).astype(jnp.float32)
    s1, i1 = lax.top_k(sc[:, :, 0], PEER_TOPK)
    s2, i2 = lax.top_k(sc[:, :, 1], PEER_TOPK)
    cand = (s1[..., :, None] + s2[..., None, :]).reshape(t, PEER_HEADS, PEER_TOPK * PEER_TOPK)
    top, idx = lax.top_k(cand, PEER_TOPK)
    e1 = jnp.take_along_axis(i1, idx // PEER_TOPK, axis=-1)
    e2 = jnp.take_along_axis(i2, idx % PEER_TOPK, axis=-1)
    experts = (e1 * PEER_N_KEYS + e2).reshape(t, PEER_HEADS * PEER_TOPK)
    gates = jax.nn.softmax(top, axis=-1).reshape(t, PEER_HEADS * PEER_TOPK)
    nblk = t // TOKEN_BLOCK

    def one_block(args):
        hb, eb, gb = args
        ug = jnp.take(u, eb, axis=0)
        vg = jnp.take(v, eb, axis=0)
        act = jax.nn.gelu(jnp.einsum("tkd,td->tk", ug, hb), approximate=False) * gb.astype(hb.dtype)
        return jnp.einsum("tk,tkd->td", act, vg)

    out = lax.map(one_block, (ht.reshape(nblk, TOKEN_BLOCK, dm),
                              experts.reshape(nblk, TOKEN_BLOCK, -1),
                              gates.reshape(nblk, TOKEN_BLOCK, -1)))
    return out.reshape(b, s, dm)


def setup_inputs(seed: int = 0) -> dict:
    key = jax.random.key(seed)
    ks = jax.random.split(key, 17)

    def nrm(k, shape, scale):
        return jax.random.normal(k, shape, jnp.float32) * scale

    return {
        "x": nrm(ks[0], (BATCH, SEQ, D_MODEL), 1.0),
        "attn_norm_g": 1.0 + nrm(ks[1], (DEPTH, D_MODEL), 0.02),
        "w_in": nrm(ks[2], (DEPTH, D_MODEL, IN_COLS), D_MODEL ** -0.5),
        "q_norm_g": 1.0 + nrm(ks[3], (DEPTH, DIFF_HEAD_DIM), 0.02),
        "k_norm_g": 1.0 + nrm(ks[4], (DEPTH, DIFF_HEAD_DIM), 0.02),
        "lambda_q1": nrm(ks[5], (DEPTH, DIFF_HEAD_DIM), 0.1),
        "lambda_k1": nrm(ks[6], (DEPTH, DIFF_HEAD_DIM), 0.1),
        "lambda_q2": nrm(ks[7], (DEPTH, DIFF_HEAD_DIM), 0.1),
        "lambda_k2": nrm(ks[8], (DEPTH, DIFF_HEAD_DIM), 0.1),
        "subln_g": 1.0 + nrm(ks[9], (DEPTH, DIFF_V_DIM), 0.02),
        "conv_w": nrm(ks[10], (DEPTH, CONV_K, CONV_WIDTH), CONV_K ** -0.5),
        "w_out": nrm(ks[11], (DEPTH, MIX_WIDTH, D_MODEL), MIX_WIDTH ** -0.5),
        "ffn_norm_g": 1.0 + nrm(ks[12], (DEPTH, D_MODEL), 0.02),
        "peer_w_q": nrm(ks[13], (DEPTH, D_MODEL, PEER_HEADS * PEER_QUERY_DIM), D_MODEL ** -0.5),
        "peer_subkeys": nrm(ks[14], (DEPTH, PEER_HEADS, 2, PEER_N_KEYS, PEER_HALF), PEER_HALF ** -0.5),
        "peer_u": nrm(ks[15], (DEPTH, PEER_N_EXPERTS, D_MODEL), D_MODEL ** -0.5),
        "peer_v": nrm(ks[16], (DEPTH, PEER_N_EXPERTS, D_MODEL), 0.5),
    }


def reference(x, attn_norm_g, w_in, q_norm_g, k_norm_g, lambda_q1, lambda_k1, lambda_q2, lambda_k2,
              subln_g, conv_w, w_out, ffn_norm_g, peer_w_q, peer_subkeys, peer_u, peer_v):
    b, s, _ = x.shape
    splits = [ATTN_WIDTH, 2 * ATTN_WIDTH, 3 * ATTN_WIDTH,
              3 * ATTN_WIDTH + CONV_WIDTH, 3 * ATTN_WIDTH + 2 * CONV_WIDTH]
    for l in range(DEPTH):
        h = _rms(x, attn_norm_g[l])
        proj = h @ w_in[l]
        q, k, v, cb, cc, cx = jnp.split(proj, splits, axis=-1)
        q = _rms(q.reshape(b, s, N_DIFF_HEADS, 2, DIFF_HEAD_DIM), q_norm_g[l])
        k = _rms(k.reshape(b, s, N_DIFF_HEADS, 2, DIFF_HEAD_DIM), k_norm_g[l])
        v = v.reshape(b, s, N_DIFF_HEADS, DIFF_V_DIM)
        lam_init = 0.8 - 0.6 * math.exp(-0.3 * l)
        lam = (jnp.exp(jnp.sum(lambda_q1[l].astype(jnp.float32) * lambda_k1[l].astype(jnp.float32)))
               - jnp.exp(jnp.sum(lambda_q2[l].astype(jnp.float32) * lambda_k2[l].astype(jnp.float32)))
               + lam_init)
        a = _diff_attention(q, k, v, lam)
        a = (_rms(a, subln_g[l]) * (1.0 - lam_init)).reshape(b, s, ATTN_WIDTH)
        c = _short_conv(cb, cc, cx, conv_w[l])
        x = x + jnp.concatenate([a, c], axis=-1) @ w_out[l]
        x = x + _peer(_rms(x, ffn_norm_g[l]), peer_w_q[l], peer_subkeys[l], peer_u[l], peer_v[l])
    return x
```

```python
import functools
import math

import jax
import jax.numpy as jnp
from jax import lax
from jax.experimental import pallas as pl
from jax.experimental.pallas import tpu as pltpu

EPS = 1e-6
HEAD_DIM = 64
V_DIM = 2 * HEAD_DIM
CONV_K = 3
PEER_HEADS = 8
PEER_KEYS = 128
PEER_HALF = 128
PEER_TOPK = 16
K_SHIFT = 4
LANES = 128
ALIBI_SPLIT = 256
VMEM_LIMIT = 56 * 1024 * 1024

F32 = jnp.float32
BF16 = jnp.bfloat16
NT_DIMS = (((1,), (1,)), ((), ()))


def _params(*sem):
    return pltpu.CompilerParams(dimension_semantics=sem, vmem_limit_bytes=VMEM_LIMIT)


def _in_proj_kernel(x_ref, g_ref, w_ref, o_ref, h_ref):
    @pl.when(pl.program_id(1) == 0)
    def _():
        x = x_ref[...]
        ms = jnp.mean(x * x, axis=-1, keepdims=True)
        h_ref[...] = (x * lax.rsqrt(ms + EPS) * g_ref[...]).astype(BF16)

    o_ref[...] = jnp.dot(h_ref[...], w_ref[...], preferred_element_type=F32).astype(o_ref.dtype)


def _in_proj(x2, g, w_bf, *, tm, tn):
    t, d = x2.shape
    n = w_bf.shape[1]
    return pl.pallas_call(
        _in_proj_kernel,
        grid=(t // tm, n // tn),
        in_specs=[pl.BlockSpec((tm, d), lambda i, j: (i, 0)),
                  pl.BlockSpec((1, d), lambda i, j: (0, 0)),
                  pl.BlockSpec((d, tn), lambda i, j: (0, j))],
        out_specs=pl.BlockSpec((tm, tn), lambda i, j: (i, j)),
        out_shape=jax.ShapeDtypeStruct((t, n), BF16),
        scratch_shapes=[pltpu.VMEM((tm, d), BF16)],
        compiler_params=_params("parallel", "arbitrary"),
        name="in_proj",
    )(x2, g, w_bf)


def _attn_kernel(q_ref, k_ref, v_ref, qg_ref, kg_ref, lq1_ref, lk1_ref, lq2_ref, lk2_ref, sg_ref,
                 o_ref, qa_ref, ka_ref, *, tq, n_heads, lam_init):
    s_len = q_ref.shape[0]
    lane = lax.broadcasted_iota(jnp.int32, (1, LANES), 1)
    first = lane < HEAD_DIM

    lam = (jnp.exp(jnp.sum(lq1_ref[...] * lk1_ref[...], axis=-1, keepdims=True))
           - jnp.exp(jnp.sum(lq2_ref[...] * lk2_ref[...], axis=-1, keepdims=True)) + lam_init)
    head = (pl.program_id(1) + 1).astype(F32)
    slope = jnp.exp2(jnp.full((1, 1), -8.0 / n_heads, F32) * head)

    def qk_norm(x_ref, g_ref):
        x = x_ref[...].astype(F32)
        x2 = x * x
        s1 = jnp.sum(jnp.where(first, x2, 0.0), axis=-1, keepdims=True)
        s2 = jnp.sum(jnp.where(first, 0.0, x2), axis=-1, keepdims=True)
        r = jnp.where(first, lax.rsqrt(s1 / HEAD_DIM + EPS), lax.rsqrt(s2 / HEAD_DIM + EPS))
        return x * r * g_ref[...]

    pos = lax.broadcasted_iota(jnp.int32, (s_len, 1), 0)
    hi = (pos & ~(ALIBI_SPLIT - 1)).astype(F32)
    lo = (pos & (ALIBI_SPLIT - 1)).astype(F32)
    one = jnp.ones((s_len, 1), F32)
    zero = jnp.zeros((s_len, 1), F32)
    q_aug = jnp.where(lane == HEAD_DIM, -slope * hi,
                      jnp.where(lane == HEAD_DIM + 1, -slope * lo,
                                jnp.where(lane <= HEAD_DIM + 3, one, zero)))
    k_aug = jnp.where(lane <= HEAD_DIM + 1, one,
                      jnp.where(lane == HEAD_DIM + 2, slope * hi,
                                jnp.where(lane == HEAD_DIM + 3, slope * lo, zero)))

    qn = qk_norm(q_ref, qg_ref) * (1.0 / math.sqrt(HEAD_DIM))
    kn = qk_norm(k_ref, kg_ref)
    qa_ref[0] = jnp.where(first, qn, q_aug).astype(BF16)
    qa_ref[1] = jnp.where(first, pltpu.roll(qn, HEAD_DIM, axis=1), q_aug).astype(BF16)
    ka_ref[0] = jnp.where(first, kn, k_aug).astype(BF16)
    ka_ref[1] = jnp.where(first, pltpu.roll(kn, HEAD_DIM, axis=1), k_aug).astype(BF16)

    row = lax.broadcasted_iota(jnp.int32, (tq, tq), 0)
    col = lax.broadcasted_iota(jnp.int32, (tq, tq), 1)
    causal = col <= row
    sg = sg_ref[...]

    for i in range(s_len // tq):
        kv = (i + 1) * tq

        def probs(c):
            s = lax.dot_general(qa_ref[c, i * tq:(i + 1) * tq, :], ka_ref[c, :kv, :], NT_DIMS,
                                preferred_element_type=F32)
            diag = jnp.where(causal, s[:, kv - tq:], -jnp.inf)
            s = diag if i == 0 else jnp.concatenate([s[:, :kv - tq], diag], axis=1)
            p = jnp.exp(s - jnp.max(s, axis=-1, keepdims=True))
            return p, jnp.sum(p, axis=-1, keepdims=True)

        p1, l1 = probs(0)
        p2, l2 = probs(1)
        a = p1 * (1.0 / l1) - p2 * (lam / l2)
        o = jnp.dot(a.astype(BF16), v_ref[:kv, :], preferred_element_type=F32)
        ms = jnp.mean(o * o, axis=-1, keepdims=True)
        o = o * lax.rsqrt(ms + EPS) * sg * (1.0 - lam_init)
        o_ref[i * tq:(i + 1) * tq, :] = o.astype(o_ref.dtype)


def _attention(proj, qg2, kg2, lq1, lk1, lq2, lk2, sg, *, batch, seq, n_heads, lam_init, tq):
    t = proj.shape[0]
    small = lambda w: pl.BlockSpec((1, w), lambda b, h: (0, 0))
    kern = functools.partial(_attn_kernel, tq=tq, n_heads=n_heads, lam_init=lam_init)
    return pl.pallas_call(
        kern,
        grid=(batch, n_heads),
        in_specs=[pl.BlockSpec((seq, V_DIM), lambda b, h: (b, h)),
                  pl.BlockSpec((seq, V_DIM), lambda b, h: (b, n_heads + h)),
                  pl.BlockSpec((seq, V_DIM), lambda b, h: (b, 2 * n_heads + h)),
                  small(V_DIM), small(V_DIM), small(HEAD_DIM), small(HEAD_DIM), small(HEAD_DIM),
                  small(HEAD_DIM), small(V_DIM)],
        out_specs=pl.BlockSpec((seq, V_DIM), lambda b, h: (b, h)),
        out_shape=jax.ShapeDtypeStruct((t, n_heads * V_DIM), BF16),
        scratch_shapes=[pltpu.VMEM((2, seq, V_DIM), BF16), pltpu.VMEM((2, seq, V_DIM), BF16)],
        compiler_params=_params("parallel", "parallel"),
        name="diff_attn",
    )(proj, proj, proj, qg2, kg2, lq1, lk1, lq2, lk2, sg)


HALO = 16


def _mix_out_kernel(x_ref, a_ref, cb_ref, cc_ref, cx_ref, hc_ref, hx_ref, cw_ref, w_ref, o_ref, u_ref,
                    *, tiles_per_seq):
    tm = x_ref.shape[0]
    i = pl.program_id(0)
    u_ref[HALO:, :] = cc_ref[...].astype(F32) * cx_ref[...].astype(F32)
    halo = hc_ref[...].astype(F32) * hx_ref[...].astype(F32)
    u_ref[:HALO, :] = jnp.where(lax.rem(i, tiles_per_seq) == 0, 0.0, halo)
    cw = cw_ref[...]
    y = u_ref[HALO - 2:HALO - 2 + tm, :] * cw[0:1, :]
    y = y + u_ref[HALO - 1:HALO - 1 + tm, :] * cw[1:2, :]
    y = y + u_ref[HALO:, :] * cw[2:3, :]
    c = (cb_ref[...].astype(F32) * y).astype(BF16)
    aw = a_ref.shape[1]
    o = jnp.dot(a_ref[...], w_ref[:aw, :], preferred_element_type=F32)
    o = o + jnp.dot(c, w_ref[aw:, :], preferred_element_type=F32)
    o_ref[...] = x_ref[...] + o


def _mix_out(x2, a, proj, conv_w, w_out_bf, *, seq, tm, attn_w, conv_wd):
    t, d = x2.shape
    cblk = (3 * attn_w) // conv_wd
    hb = tm // HALO
    halo_map = lambda off: (lambda i: (jnp.maximum(i * hb - 1, 0), cblk + off))
    kern = functools.partial(_mix_out_kernel, tiles_per_seq=seq // tm)
    return pl.pallas_call(
        kern,
        grid=(t // tm,),
        in_specs=[pl.BlockSpec((tm, d), lambda i: (i, 0)),
                  pl.BlockSpec((tm, attn_w), lambda i: (i, 0)),
                  pl.BlockSpec((tm, conv_wd), lambda i: (i, cblk)),
                  pl.BlockSpec((tm, conv_wd), lambda i: (i, cblk + 1)),
                  pl.BlockSpec((tm, conv_wd), lambda i: (i, cblk + 2)),
                  pl.BlockSpec((HALO, conv_wd), halo_map(1)),
                  pl.BlockSpec((HALO, conv_wd), halo_map(2)),
                  pl.BlockSpec((CONV_K, conv_wd), lambda i: (0, 0)),
                  pl.BlockSpec((attn_w + conv_wd, d), lambda i: (0, 0))],
        out_specs=pl.BlockSpec((tm, d), lambda i: (i, 0)),
        out_shape=jax.ShapeDtypeStruct((t, d), F32),
        scratch_shapes=[pltpu.VMEM((tm + HALO, conv_wd), F32)],
        compiler_params=_params("parallel"),
        name="mix_out",
    )(x2, a, proj, proj, proj, proj, proj, conv_w, w_out_bf)


def _top_rows(work, k):
    n, tm = work.shape
    ridx = lax.broadcasted_iota(jnp.int32, (n, tm), 0)
    krow = lax.broadcasted_iota(jnp.int32, (k, tm), 0)
    vals = jnp.zeros((k, tm), F32)
    idxs = jnp.zeros((k, tm), jnp.int32)
    for r in range(k):
        m = jnp.max(work, axis=0, keepdims=True)
        am = jnp.min(jnp.where(work == m, ridx, n), axis=0, keepdims=True)
        vals = jnp.where(krow == r, m, vals)
        idxs = jnp.where(krow == r, am, idxs)
        work = jnp.where(ridx == am, -jnp.inf, work)
    return vals, idxs


def _select_rows(table, sel, k):
    out = jnp.zeros(sel.shape, table.dtype)
    for i in range(k):
        out = jnp.where(sel == i, table[i:i + 1, :], out)
    return out


def _route_kernel(x_ref, g_ref, wq_ref, sk_ref, h_ref, e_ref, gate_ref):
    x = x_ref[...]
    ms = jnp.mean(x * x, axis=-1, keepdims=True)
    h = x * lax.rsqrt(ms + EPS) * g_ref[...]
    h_ref[...] = h
    q = jnp.dot(h.astype(BF16), wq_ref[...], preferred_element_type=F32)
    k = PEER_TOPK
    experts, gates = [], []
    for hd in range(PEER_HEADS):
        sv, si = [], []
        for c in range(2):
            g = hd * 2 + c
            qg = q[:, g * PEER_HALF:(g + 1) * PEER_HALF].astype(BF16)
            sc = lax.dot_general(sk_ref[g], qg, NT_DIMS, preferred_element_type=F32)
            v, ix = _top_rows(sc, k)
            sv.append(v)
            si.append(ix)
        cand = jnp.concatenate([sv[0][i:i + 1, :] + sv[1] for i in range(k)], axis=0)
        top, idx = _top_rows(cand, k)
        e1 = _select_rows(si[0], lax.shift_right_logical(idx, K_SHIFT), k)
        e2 = _select_rows(si[1], idx & (k - 1), k)
        experts.append(e1 * PEER_KEYS + e2)
        p = jnp.exp(top - top[0:1, :])
        gates.append(p / jnp.sum(p, axis=0, keepdims=True))
    e_ref[...] = jnp.concatenate(experts, axis=0).T
    gate_ref[...] = jnp.concatenate(gates, axis=0).T


def _route(x1, g, wq_bf, sk_bf, *, tm):
    t, d = x1.shape
    nq = wq_bf.shape[1]
    ng, nk, nh = sk_bf.shape
    ek = PEER_HEADS * PEER_TOPK
    return pl.pallas_call(
        _route_kernel,
        grid=(t // tm,),
        in_specs=[pl.BlockSpec((tm, d), lambda i: (i, 0)),
                  pl.BlockSpec((1, d), lambda i: (0, 0)),
                  pl.BlockSpec((d, nq), lambda i: (0, 0)),
                  pl.BlockSpec((ng, nk, nh), lambda i: (0, 0, 0))],
        out_specs=[pl.BlockSpec((tm, d), lambda i: (i, 0)),
                   pl.BlockSpec((tm, ek), lambda i: (i, 0)),
                   pl.BlockSpec((tm, ek), lambda i: (i, 0))],
        out_shape=[jax.ShapeDtypeStruct((t, d), F32),
                   jax.ShapeDtypeStruct((t, ek), jnp.int32),
                   jax.ShapeDtypeStruct((t, ek), F32)],
        compiler_params=_params("parallel"),
        name="peer_route",
    )(x1, g, wq_bf, sk_bf)


def _gelu(x):
    return 0.5 * x * (1.0 + lax.erf(x * (1.0 / math.sqrt(2.0))))


def _peer_kernel(e_ref, g2_ref, h_ref, x_ref, tab_ref, o_ref, buf_ref, sem_ref, *, n_buf):
    tb, ek = e_ref.shape
    half = tab_ref.shape[1] // 2
    even = lax.broadcasted_iota(jnp.int32, (1, 2 * ek), 1) % 2 == 0

    def row_copy(expert, slot, k):
        return pltpu.make_async_copy(tab_ref.at[pl.ds(expert, 1)], buf_ref.at[slot, pl.ds(k, 1)],
                                     sem_ref.at[slot])

    def issue(t, slot):
        for k in range(ek):
            row_copy(e_ref[t, k], slot, k).start()

    def wait(slot):
        pltpu.make_async_copy(tab_ref.at[pl.ds(0, ek)], buf_ref.at[slot], sem_ref.at[slot]).wait()

    for j in range(n_buf - 1):
        issue(j, j)

    def body(t, carry):
        nxt = t + (n_buf - 1)

        @pl.when(nxt < tb)
        def _():
            issue(nxt, nxt % n_buf)

        slot = t % n_buf
        wait(slot)
        rows = pltpu.bitcast(buf_ref[slot], BF16)
        hrow = h_ref[pl.ds(t, 1), :].astype(BF16)
        hh = jnp.concatenate([hrow[:, :half], hrow[:, half:]], axis=0)
        r = lax.dot_general(hh, rows[:, :half], NT_DIMS, preferred_element_type=F32)
        s = r[0:1, :] + pltpu.roll(r[1:2, :], 2 * ek - 1, axis=1)
        act = jnp.where(even, _gelu(s) * g2_ref[pl.ds(t, 1), :], 0.0)
        a2 = jnp.concatenate([act, pltpu.roll(act, 1, axis=1)], axis=0).astype(BF16)
        o = jnp.dot(a2, rows[:, half:], preferred_element_type=F32)
        o_ref[pl.ds(t, 1), :half] = x_ref[pl.ds(t, 1), :half] + o[0:1, :]
        o_ref[pl.ds(t, 1), half:] = x_ref[pl.ds(t, 1), half:] + o[1:2, :]
        return carry

    lax.fori_loop(0, tb, body, 0)


def _peer(experts, gates2, h, x1, table, *, tb, n_buf):
    t, d = x1.shape
    ek = experts.shape[1]
    kern = functools.partial(_peer_kernel, n_buf=n_buf)
    return pl.pallas_call(
        kern,
        grid=(t // tb,),
        in_specs=[pl.BlockSpec((tb, ek), lambda i: (i, 0), memory_space=pltpu.SMEM),
                  pl.BlockSpec((tb, 2 * ek), lambda i: (i, 0)),
                  pl.BlockSpec((tb, d), lambda i: (i, 0)),
                  pl.BlockSpec((tb, d), lambda i: (i, 0)),
                  pl.BlockSpec(memory_space=pl.ANY)],
        out_specs=pl.BlockSpec((tb, d), lambda i: (i, 0)),
        out_shape=jax.ShapeDtypeStruct((t, d), F32),
        scratch_shapes=[pltpu.VMEM((n_buf, ek, table.shape[1]), jnp.uint32),
                        pltpu.SemaphoreType.DMA((n_buf,))],
        compiler_params=_params("arbitrary"),
        name="peer_experts",
    )(experts, gates2, h, x1, table)


def _pack_table(u, v):
    half = u.shape[1] // 2

    def pack(w):
        bits = lax.bitcast_convert_type(w.astype(BF16), jnp.uint16).astype(jnp.uint32)
        return bits[:, :half] | (bits[:, half:] << 16)

    return jnp.concatenate([pack(u), pack(v)], axis=1)


def _layer(x2, l, attn_norm_g, w_in, q_norm_g, k_norm_g, lambda_q1, lambda_k1, lambda_q2, lambda_k2,
           subln_g, conv_w, w_out, ffn_norm_g, peer_w_q, peer_subkeys, peer_u, peer_v, *, batch, seq):
    t, d = x2.shape
    attn_w = d // 2
    conv_wd = d - attn_w
    n_heads = attn_w // V_DIM
    lam_init = 0.8 - 0.6 * math.exp(-0.3 * l)
    row = lambda p: p[l].reshape(1, -1).astype(F32)
    row2 = lambda p: jnp.tile(p[l].reshape(1, -1).astype(F32), (1, 2))

    proj = _in_proj(x2, row(attn_norm_g), w_in[l].astype(BF16), tm=512, tn=1536)
    a = _attention(proj, row2(q_norm_g), row2(k_norm_g), row(lambda_q1), row(lambda_k1), row(lambda_q2),
                   row(lambda_k2), row(subln_g), batch=batch, seq=seq, n_heads=n_heads,
                   lam_init=lam_init, tq=256)
    x1 = _mix_out(x2, a, proj, conv_w[l].astype(F32), w_out[l].astype(BF16), seq=seq, tm=256,
                  attn_w=attn_w, conv_wd=conv_wd)
    sk = peer_subkeys[l].reshape(PEER_HEADS * 2, PEER_KEYS, PEER_HALF).astype(BF16)
    h, experts, gates = _route(x1, row(ffn_norm_g), peer_w_q[l].astype(BF16), sk, tm=256)
    gates2 = jnp.stack([gates, jnp.zeros_like(gates)], axis=-1).reshape(t, -1)
    table = _pack_table(peer_u[l], peer_v[l])
    return _peer(experts, gates2, h, x1, table, tb=64, n_buf=4)


def kernel(x, attn_norm_g, w_in, q_norm_g, k_norm_g, lambda_q1, lambda_k1, lambda_q2, lambda_k2, subln_g,
           conv_w, w_out, ffn_norm_g, peer_w_q, peer_subkeys, peer_u, peer_v):
    b, s, d = x.shape
    x2 = x.reshape(b * s, d)
    for l in range(w_in.shape[0]):
        x2 = _layer(x2, l, attn_norm_g, w_in, q_norm_g, k_norm_g, lambda_q1, lambda_k1, lambda_q2,
                    lambda_k2, subln_g, conv_w, w_out, ffn_norm_g, peer_w_q, peer_subkeys, peer_u, peer_v,
                    batch=b, seq=s)
    return x2.reshape(b, s, d)
```

```python
import functools
import math

import jax
import jax.numpy as jnp
from jax import lax
from jax.experimental import pallas as pl
from jax.experimental.pallas import tpu as pltpu

EPS = 1e-6
HEAD_DIM = 64
V_DIM = 2 * HEAD_DIM
CONV_K = 3
PEER_HEADS = 8
PEER_KEYS = 128
PEER_HALF = 128
PEER_TOPK = 16
K_SHIFT = 4
LANES = 128
ALIBI_SPLIT = 256
VMEM_LIMIT = 56 * 1024 * 1024

F32 = jnp.float32
BF16 = jnp.bfloat16
NT_DIMS = (((1,), (1,)), ((), ()))


def _params(*sem):
    return pltpu.CompilerParams(dimension_semantics=sem, vmem_limit_bytes=VMEM_LIMIT)


def _in_proj_kernel(x_ref, g_ref, w_ref, o_ref, h_ref):
    @pl.when(pl.program_id(1) == 0)
    def _():
        x = x_ref[...]
        ms = jnp.mean(x * x, axis=-1, keepdims=True)
        h_ref[...] = (x * lax.rsqrt(ms + EPS) * g_ref[...]).astype(BF16)

    o_ref[...] = jnp.dot(h_ref[...], w_ref[...], preferred_element_type=F32).astype(o_ref.dtype)


def _in_proj(x2, g, w_bf, *, tm, tn):
    t, d = x2.shape
    n = w_bf.shape[1]
    return pl.pallas_call(
        _in_proj_kernel,
        grid=(t // tm, n // tn),
        in_specs=[pl.BlockSpec((tm, d), lambda i, j: (i, 0)),
                  pl.BlockSpec((1, d), lambda i, j: (0, 0)),
                  pl.BlockSpec((d, tn), lambda i, j: (0, j))],
        out_specs=pl.BlockSpec((tm, tn), lambda i, j: (i, j)),
        out_shape=jax.ShapeDtypeStruct((t, n), BF16),
        scratch_shapes=[pltpu.VMEM((tm, d), BF16)],
        compiler_params=_params("parallel", "arbitrary"),
        name="in_proj",
    )(x2, g, w_bf)


def _attn_kernel(q_ref, k_ref, v_ref, qg_ref, kg_ref, lq1_ref, lk1_ref, lq2_ref, lk2_ref, sg_ref,
                 o_ref, qa_ref, ka_ref, *, tq, n_heads, lam_init):
    s_len = q_ref.shape[0]
    lane = lax.broadcasted_iota(jnp.int32, (1, LANES), 1)
    first = lane < HEAD_DIM

    lam = (jnp.exp(jnp.sum(lq1_ref[...] * lk1_ref[...], axis=-1, keepdims=True))
           - jnp.exp(jnp.sum(lq2_ref[...] * lk2_ref[...], axis=-1, keepdims=True)) + lam_init)
    head = (pl.program_id(1) + 1).astype(F32)
    slope = jnp.exp2(jnp.full((1, 1), -8.0 / n_heads, F32) * head)

    def qk_norm(x_ref, g_ref):
        x = x_ref[...].astype(F32)
        x2 = x * x
        s1 = jnp.sum(jnp.where(first, x2, 0.0), axis=-1, keepdims=True)
        s2 = jnp.sum(jnp.where(first, 0.0, x2), axis=-1, keepdims=True)
        r = jnp.where(first, lax.rsqrt(s1 / HEAD_DIM + EPS), lax.rsqrt(s2 / HEAD_DIM + EPS))
        return x * r * g_ref[...]

    pos = lax.broadcasted_iota(jnp.int32, (s_len, 1), 0)
    hi = (pos & ~(ALIBI_SPLIT - 1)).astype(F32)
    lo = (pos & (ALIBI_SPLIT - 1)).astype(F32)
    one = jnp.ones((s_len, 1), F32)
    zero = jnp.zeros((s_len, 1), F32)
    q_aug = jnp.where(lane == HEAD_DIM, -slope * hi,
                      jnp.where(lane == HEAD_DIM + 1, -slope * lo,
                                jnp.where(lane <= HEAD_DIM + 3, one, zero)))
    k_aug = jnp.where(lane <= HEAD_DIM + 1, one,
                      jnp.where(lane == HEAD_DIM + 2, slope * hi,
                                jnp.where(lane == HEAD_DIM + 3, slope * lo, zero)))

    qn = qk_norm(q_ref, qg_ref) * (1.0 / math.sqrt(HEAD_DIM))
    kn = qk_norm(k_ref, kg_ref)
    qa_ref[0] = jnp.where(first, qn, q_aug).astype(BF16)
    qa_ref[1] = jnp.where(first, pltpu.roll(qn, HEAD_DIM, axis=1), q_aug).astype(BF16)
    ka_ref[0] = jnp.where(first, kn, k_aug).astype(BF16)
    ka_ref[1] = jnp.where(first, pltpu.roll(kn, HEAD_DIM, axis=1), k_aug).astype(BF16)

    row = lax.broadcasted_iota(jnp.int32, (tq, tq), 0)
    col = lax.broadcasted_iota(jnp.int32, (tq, tq), 1)
    causal = col <= row
    sg = sg_ref[...]

    for i in range(s_len // tq):
        kv = (i + 1) * tq

        def probs(c):
            s = lax.dot_general(qa_ref[c, i * tq:(i + 1) * tq, :], ka_ref[c, :kv, :], NT_DIMS,
                                preferred_element_type=F32)
            diag = jnp.where(causal, s[:, kv - tq:], -jnp.inf)
            s = diag if i == 0 else jnp.concatenate([s[:, :kv - tq], diag], axis=1)
            p = jnp.exp(s - jnp.max(s, axis=-1, keepdims=True))
            return p, jnp.sum(p, axis=-1, keepdims=True)

        p1, l1 = probs(0)
        p2, l2 = probs(1)
        a = p1 * (1.0 / l1) - p2 * (lam / l2)
        o = jnp.dot(a.astype(BF16), v_ref[:kv, :], preferred_element_type=F32)
        ms = jnp.mean(o * o, axis=-1, keepdims=True)
        o = o * lax.rsqrt(ms + EPS) * sg * (1.0 - lam_init)
        o_ref[i * tq:(i + 1) * tq, :] = o.astype(o_ref.dtype)


def _attention(proj, qg2, kg2, lq1, lk1, lq2, lk2, sg, *, batch, seq, n_heads, lam_init, tq):
    t = proj.shape[0]
    small = lambda w: pl.BlockSpec((1, w), lambda b, h: (0, 0))
    kern = functools.partial(_attn_kernel, tq=tq, n_heads=n_heads, lam_init=lam_init)
    return pl.pallas_call(
        kern,
        grid=(batch, n_heads),
        in_specs=[pl.BlockSpec((seq, V_DIM), lambda b, h: (b, h)),
                  pl.BlockSpec((seq, V_DIM), lambda b, h: (b, n_heads + h)),
                  pl.BlockSpec((seq, V_DIM), lambda b, h: (b, 2 * n_heads + h)),
                  small(V_DIM), small(V_DIM), small(HEAD_DIM), small(HEAD_DIM), small(HEAD_DIM),
                  small(HEAD_DIM), small(V_DIM)],
        out_specs=pl.BlockSpec((seq, V_DIM), lambda b, h: (b, h)),
        out_shape=jax.ShapeDtypeStruct((t, n_heads * V_DIM), BF16),
        scratch_shapes=[pltpu.VMEM((2, seq, V_DIM), BF16), pltpu.VMEM((2, seq, V_DIM), BF16)],
        compiler_params=_params("parallel", "parallel"),
        name="diff_attn",
    )(proj, proj, proj, qg2, kg2, lq1, lk1, lq2, lk2, sg)


PEER_CHUNKS = 4
WAIT_GROUP = 8
SMEM_ROWS = 16
HALO = 16


def _mix_out_kernel(x_ref, a_ref, cb_ref, cc_ref, cx_ref, hc_ref, hx_ref, cw_ref, w_ref, o_ref, u_ref,
                    *, tiles_per_seq):
    tm = x_ref.shape[0]
    i = pl.program_id(0)
    u_ref[HALO:, :] = cc_ref[...].astype(F32) * cx_ref[...].astype(F32)
    halo = hc_ref[...].astype(F32) * hx_ref[...].astype(F32)
    u_ref[:HALO, :] = jnp.where(lax.rem(i, tiles_per_seq) == 0, 0.0, halo)
    cw = cw_ref[...]
    y = u_ref[HALO - 2:HALO - 2 + tm, :] * cw[0:1, :]
    y = y + u_ref[HALO - 1:HALO - 1 + tm, :] * cw[1:2, :]
    y = y + u_ref[HALO:, :] * cw[2:3, :]
    c = (cb_ref[...].astype(F32) * y).astype(BF16)
    aw = a_ref.shape[1]
    o = jnp.dot(a_ref[...], w_ref[:aw, :], preferred_element_type=F32)
    o = o + jnp.dot(c, w_ref[aw:, :], preferred_element_type=F32)
    o_ref[...] = x_ref[...] + o


def _mix_out(x2, a, proj, conv_w, w_out_bf, *, seq, tm, attn_w, conv_wd):
    t, d = x2.shape
    cblk = (3 * attn_w) // conv_wd
    hb = tm // HALO
    halo_map = lambda off: (lambda i: (jnp.maximum(i * hb - 1, 0), cblk + off))
    kern = functools.partial(_mix_out_kernel, tiles_per_seq=seq // tm)
    return pl.pallas_call(
        kern,
        grid=(t // tm,),
        in_specs=[pl.BlockSpec((tm, d), lambda i: (i, 0)),
                  pl.BlockSpec((tm, attn_w), lambda i: (i, 0)),
                  pl.BlockSpec((tm, conv_wd), lambda i: (i, cblk)),
                  pl.BlockSpec((tm, conv_wd), lambda i: (i, cblk + 1)),
                  pl.BlockSpec((tm, conv_wd), lambda i: (i, cblk + 2)),
                  pl.BlockSpec((HALO, conv_wd), halo_map(1)),
                  pl.BlockSpec((HALO, conv_wd), halo_map(2)),
                  pl.BlockSpec((CONV_K, conv_wd), lambda i: (0, 0)),
                  pl.BlockSpec((attn_w + conv_wd, d), lambda i: (0, 0))],
        out_specs=pl.BlockSpec((tm, d), lambda i: (i, 0)),
        out_shape=jax.ShapeDtypeStruct((t, d), F32),
        scratch_shapes=[pltpu.VMEM((tm + HALO, conv_wd), F32)],
        compiler_params=_params("parallel"),
        name="mix_out",
    )(x2, a, proj, proj, proj, proj, proj, conv_w, w_out_bf)


def _top_rows(work, k):
    n, tm = work.shape
    ridx = lax.broadcasted_iota(jnp.int32, (n, tm), 0)
    krow = lax.broadcasted_iota(jnp.int32, (k, tm), 0)
    vals = jnp.zeros((k, tm), F32)
    idxs = jnp.zeros((k, tm), jnp.int32)
    for r in range(k):
        m = jnp.max(work, axis=0, keepdims=True)
        am = jnp.min(jnp.where(work == m, ridx, n), axis=0, keepdims=True)
        vals = jnp.where(krow == r, m, vals)
        idxs = jnp.where(krow == r, am, idxs)
        work = jnp.where(ridx == am, -jnp.inf, work)
    return vals, idxs


def _select_rows(table, sel, k):
    out = jnp.zeros(sel.shape, table.dtype)
    for i in range(k):
        out = jnp.where(sel == i, table[i:i + 1, :], out)
    return out


def _route_kernel(x_ref, g_ref, wq_ref, sk_ref, h_ref, e_ref, gate_ref):
    x = x_ref[...]
    ms = jnp.mean(x * x, axis=-1, keepdims=True)
    h = x * lax.rsqrt(ms + EPS) * g_ref[...]
    h_ref[...] = h
    q = jnp.dot(h.astype(BF16), wq_ref[...], preferred_element_type=F32)
    k = PEER_TOPK
    experts, gates = [], []
    for hd in range(PEER_HEADS):
        sv, si = [], []
        for c in range(2):
            g = hd * 2 + c
            qg = q[:, g * PEER_HALF:(g + 1) * PEER_HALF].astype(BF16)
            sc = lax.dot_general(sk_ref[g], qg, NT_DIMS, preferred_element_type=F32)
            v, ix = _top_rows(sc, k)
            sv.append(v)
            si.append(ix)
        cand = jnp.concatenate([sv[0][i:i + 1, :] + sv[1] for i in range(k)], axis=0)
        top, idx = _top_rows(cand, k)
        e1 = _select_rows(si[0], lax.shift_right_logical(idx, K_SHIFT), k)
        e2 = _select_rows(si[1], idx & (k - 1), k)
        experts.append(e1 * PEER_KEYS + e2)
        p = jnp.exp(top - top[0:1, :])
        gates.append(p / jnp.sum(p, axis=0, keepdims=True))
    e_ref[...] = jnp.concatenate(experts, axis=0).T
    gate_ref[...] = jnp.concatenate(gates, axis=0).T


def _route(x1, g, wq_bf, sk_bf, *, tm):
    t, d = x1.shape
    nq = wq_bf.shape[1]
    ng, nk, nh = sk_bf.shape
    ek = PEER_HEADS * PEER_TOPK
    return pl.pallas_call(
        _route_kernel,
        grid=(t // tm,),
        in_specs=[pl.BlockSpec((tm, d), lambda i: (i, 0)),
                  pl.BlockSpec((1, d), lambda i: (0, 0)),
                  pl.BlockSpec((d, nq), lambda i: (0, 0)),
                  pl.BlockSpec((ng, nk, nh), lambda i: (0, 0, 0))],
        out_specs=[pl.BlockSpec((tm, d), lambda i: (i, 0)),
                   pl.BlockSpec((tm, ek), lambda i: (i, 0)),
                   pl.BlockSpec((tm, ek), lambda i: (i, 0))],
        out_shape=[jax.ShapeDtypeStruct((t, d), F32),
                   jax.ShapeDtypeStruct((t, ek), jnp.int32),
                   jax.ShapeDtypeStruct((t, ek), F32)],
        compiler_params=_params("parallel"),
        name="peer_route",
    )(x1, g, wq_bf, sk_bf)


def _gelu(x):
    return 0.5 * x * (1.0 + lax.erf(x * (1.0 / math.sqrt(2.0))))


def _peer_kernel(e_ref, en_ref, g2_ref, h_ref, x_ref, tab_ref, o_ref, buf_ref, sem_ref, *, n_buf):
    tb, ek = e_ref.shape
    half = tab_ref.shape[2] // 2
    step = pl.program_id(0)
    last_step = pl.num_programs(0) - 1
    even = lax.broadcasted_iota(jnp.int32, (1, 2 * ek), 1) % 2 == 0

    cw = half // PEER_CHUNKS
    per = ek // PEER_CHUNKS

    def issue(idx_ref, t, slot, k0, k1):
        for k in range(k0, k1):
            pltpu.make_async_copy(tab_ref.at[idx_ref[t, k]], buf_ref.at[slot, pl.ds(k, 1)],
                                  sem_ref.at[slot]).start()

    def wait(slot):
        pltpu.make_async_copy(tab_ref.at[pl.ds(0, ek), 0], buf_ref.at[slot], sem_ref.at[slot]).wait()

    def store(t, o):
        o_ref[pl.ds(t, 1), :half] = x_ref[pl.ds(t, 1), :half] + o[0:1, :]
        o_ref[pl.ds(t, 1), half:] = x_ref[pl.ds(t, 1), half:] + o[1:2, :]

    def s1_lhs(t):
        hrow = h_ref[pl.ds(t, 1), :].astype(BF16)
        return jnp.concatenate([hrow[:, :half], hrow[:, half:]], axis=0)

    def gate_act(r, t):
        s = r[0:1, :] + pltpu.roll(r[1:2, :], 2 * ek - 1, axis=1)
        act = jnp.where(even, _gelu(s) * g2_ref[pl.ds(t, 1), :], 0.0)
        return jnp.concatenate([act, pltpu.roll(act, 1, axis=1)], axis=0).astype(BF16)

    def body(t, slot, carry, *, s1=True, act=True, s2=True, row_out=None, nxt=None):
        r_prev, a2_prev, o_prev = carry
        if s1:
            if slot % WAIT_GROUP == 0:
                for w in range(WAIT_GROUP):
                    wait(slot + w)
            hh = s1_lhs(t)
        a2 = gate_act(r_prev, t - 1) if act else a2_prev
        slot_v = (slot - 2) % n_buf
        r = jnp.zeros((2, 2 * ek), F32)
        outs = []
        for c in range(PEER_CHUNKS):
            if s1:
                rows = pltpu.bitcast(buf_ref[slot, :, c * cw:(c + 1) * cw], BF16)
                r = r + lax.dot_general(hh[:, c * cw:(c + 1) * cw], rows, NT_DIMS,
                                        preferred_element_type=F32)
            if s2:
                rows = pltpu.bitcast(buf_ref[slot_v, :, half + c * cw:half + (c + 1) * cw], BF16)
                outs.append(jnp.dot(a2_prev, rows, preferred_element_type=F32))
            if nxt is not None:
                issue(*nxt, c * per, (c + 1) * per)
        if row_out is not None:
            store(row_out, o_prev)
        return r, a2, (jnp.concatenate(outs, axis=1) if s2 else o_prev)

    look = n_buf - 3

    @pl.when(step == 0)
    def _():
        for j in range(look):
            issue(e_ref, j, j, 0, ek)

    def future(t):
        tn = t + look
        return (e_ref, tn, tn % n_buf) if tn < tb else (en_ref, tn - tb, tn % n_buf)

    carry = (jnp.zeros((2, 2 * ek), F32), jnp.zeros((2, 2 * ek), BF16), jnp.zeros((2, half), F32))
    carry = body(0, 0, carry, act=False, s2=False, nxt=future(0))
    carry = body(1, 1, carry, s2=False, nxt=future(1))

    def group(g, carry):
        for j in range(n_buf):
            t = 2 + g * n_buf + j
            slot = (2 + j) % n_buf
            row_out = t - 3 if j else jnp.maximum(t - 3, 0)
            carry = body(t, slot, carry, row_out=row_out, nxt=(e_ref, t + look, (slot + look) % n_buf))
        return carry

    n_groups = (tb - 2) // n_buf
    assert n_groups * n_buf + 1 + look < tb
    carry = lax.fori_loop(0, n_groups, group, carry)
    for t in range(2 + n_groups * n_buf, tb):
        carry = body(t, t % n_buf, carry, row_out=t - 3, nxt=future(t))
    carry = body(tb, tb % n_buf, carry, s1=False, row_out=tb - 3)
    carry = body(tb + 1, (tb + 1) % n_buf, carry, s1=False, act=False, row_out=tb - 2)
    store(tb - 1, carry[2])

    @pl.when(step == last_step)
    def _():
        for j in range(look):
            wait(j)


def _peer(experts, gates2, h, x1, table, *, tb, n_buf):
    t, d = x1.shape
    ek = experts.shape[1]
    n_steps = t // tb
    assert tb % n_buf == 0 and n_buf - 3 <= SMEM_ROWS and tb % SMEM_ROWS == 0 and n_buf % WAIT_GROUP == 0
    nb = tb // SMEM_ROWS
    kern = functools.partial(_peer_kernel, n_buf=n_buf)
    return pl.pallas_call(
        kern,
        grid=(n_steps,),
        in_specs=[pl.BlockSpec((tb, ek), lambda i: (i, 0), memory_space=pltpu.SMEM),
                  pl.BlockSpec((SMEM_ROWS, ek), lambda i: (jnp.minimum(i + 1, n_steps - 1) * nb, 0),
                               memory_space=pltpu.SMEM),
                  pl.BlockSpec((tb, 2 * ek), lambda i: (i, 0)),
                  pl.BlockSpec((tb, d), lambda i: (i, 0)),
                  pl.BlockSpec((tb, d), lambda i: (i, 0)),
                  pl.BlockSpec(memory_space=pl.ANY)],
        out_specs=pl.BlockSpec((tb, d), lambda i: (i, 0)),
        out_shape=jax.ShapeDtypeStruct((t, d), F32),
        scratch_shapes=[pltpu.VMEM((n_buf, ek, table.shape[2]), jnp.uint32),
                        pltpu.SemaphoreType.DMA((n_buf,))],
        compiler_params=_params("arbitrary"),
        name="peer_experts",
    )(experts, experts, gates2, h, x1, table)


def _pack_table(u, v):
    half = u.shape[1] // 2

    def pack(w):
        bits = lax.bitcast_convert_type(w.astype(BF16), jnp.uint16).astype(jnp.uint32)
        return bits[:, :half] | (bits[:, half:] << 16)

    return jnp.concatenate([pack(u), pack(v)], axis=1)[:, None, :]


def _layer(x2, l, attn_norm_g, w_in, q_norm_g, k_norm_g, lambda_q1, lambda_k1, lambda_q2, lambda_k2,
           subln_g, conv_w, w_out, ffn_norm_g, peer_w_q, peer_subkeys, peer_u, peer_v, *, batch, seq):
    t, d = x2.shape
    attn_w = d // 2
    conv_wd = d - attn_w
    n_heads = attn_w // V_DIM
    lam_init = 0.8 - 0.6 * math.exp(-0.3 * l)
    row = lambda p: p[l].reshape(1, -1).astype(F32)
    row2 = lambda p: jnp.tile(p[l].reshape(1, -1).astype(F32), (1, 2))

    proj = _in_proj(x2, row(attn_norm_g), w_in[l].astype(BF16), tm=512, tn=1536)
    a = _attention(proj, row2(q_norm_g), row2(k_norm_g), row(lambda_q1), row(lambda_k1), row(lambda_q2),
                   row(lambda_k2), row(subln_g), batch=batch, seq=seq, n_heads=n_heads,
                   lam_init=lam_init, tq=256)
    x1 = _mix_out(x2, a, proj, conv_w[l].astype(F32), w_out[l].astype(BF16), seq=seq, tm=256,
                  attn_w=attn_w, conv_wd=conv_wd)
    sk = peer_subkeys[l].reshape(PEER_HEADS * 2, PEER_KEYS, PEER_HALF).astype(BF16)
    h, experts, gates = _route(x1, row(ffn_norm_g), peer_w_q[l].astype(BF16), sk, tm=256)
    gates2 = jnp.stack([gates, jnp.zeros_like(gates)], axis=-1).reshape(t, -1)
    table = _pack_table(peer_u[l], peer_v[l])
    return _peer(experts, gates2, h, x1, table, tb=64, n_buf=16)


def kernel(x, attn_norm_g, w_in, q_norm_g, k_norm_g, lambda_q1, lambda_k1, lambda_q2, lambda_k2, subln_g,
           conv_w, w_out, ffn_norm_g, peer_w_q, peer_subkeys, peer_u, peer_v):
    b, s, d = x.shape
    x2 = x.reshape(b * s, d)
    for l in range(w_in.shape[0]):
        x2 = _layer(x2, l, attn_norm_g, w_in, q_norm_g, k_norm_g, lambda_q1, lambda_k1, lambda_q2,
                    lambda_k2, subln_g, conv_w, w_out, ffn_norm_g, peer_w_q, peer_subkeys, peer_u, peer_v,
                    batch=b, seq=s)
    return x2.reshape(b, s, d)
```

```python
import functools
import math

import jax
import jax.numpy as jnp
from jax import lax
from jax.experimental import pallas as pl
from jax.experimental.pallas import tpu as pltpu

EPS = 1e-6
HEAD_DIM = 64
V_DIM = 2 * HEAD_DIM
CONV_K = 3
PEER_HEADS = 8
PEER_KEYS = 128
PEER_HALF = 128
PEER_TOPK = 16
K_SHIFT = 4
LANES = 128
SUBLANES = 8
ALIBI_SPLIT = 256
VMEM_LIMIT = 56 * 1024 * 1024

F32 = jnp.float32
BF16 = jnp.bfloat16
NT_DIMS = (((1,), (1,)), ((), ()))


def _params(*sem):
    return pltpu.CompilerParams(dimension_semantics=sem, vmem_limit_bytes=VMEM_LIMIT)


def _in_proj_kernel(x_ref, g_ref, w_ref, o_ref, h_ref):
    @pl.when(pl.program_id(1) == 0)
    def _():
        x = x_ref[...]
        ms = jnp.mean(x * x, axis=-1, keepdims=True)
        h_ref[...] = (x * lax.rsqrt(ms + EPS) * g_ref[...]).astype(BF16)

    o_ref[...] = jnp.dot(h_ref[...], w_ref[...], preferred_element_type=F32).astype(o_ref.dtype)


def _in_proj(x2, g, w_bf, *, tm, tn):
    t, d = x2.shape
    n = w_bf.shape[1]
    return pl.pallas_call(
        _in_proj_kernel,
        grid=(t // tm, n // tn),
        in_specs=[pl.BlockSpec((tm, d), lambda i, j: (i, 0)),
                  pl.BlockSpec((1, d), lambda i, j: (0, 0)),
                  pl.BlockSpec((d, tn), lambda i, j: (0, j))],
        out_specs=pl.BlockSpec((tm, tn), lambda i, j: (i, j)),
        out_shape=jax.ShapeDtypeStruct((t, n), BF16),
        scratch_shapes=[pltpu.VMEM((tm, d), BF16)],
        compiler_params=_params("parallel", "arbitrary"),
        name="in_proj",
    )(x2, g, w_bf)


def _attn_kernel(q_ref, k_ref, v_ref, qg_ref, kg_ref, lq1_ref, lk1_ref, lq2_ref, lk2_ref, sg_ref,
                 o_ref, qa_ref, ka_ref, *, tq, n_heads, lam_init):
    s_len = q_ref.shape[0]
    lane = lax.broadcasted_iota(jnp.int32, (1, LANES), 1)
    first = lane < HEAD_DIM

    lam = (jnp.exp(jnp.sum(lq1_ref[...] * lk1_ref[...], axis=-1, keepdims=True))
           - jnp.exp(jnp.sum(lq2_ref[...] * lk2_ref[...], axis=-1, keepdims=True)) + lam_init)
    head = (pl.program_id(1) + 1).astype(F32)
    slope = jnp.exp2(jnp.full((1, 1), -8.0 / n_heads, F32) * head)

    def qk_norm(x_ref, g_ref):
        x = x_ref[...].astype(F32)
        x2 = x * x
        s1 = jnp.sum(jnp.where(first, x2, 0.0), axis=-1, keepdims=True)
        s2 = jnp.sum(jnp.where(first, 0.0, x2), axis=-1, keepdims=True)
        r = jnp.where(first, lax.rsqrt(s1 / HEAD_DIM + EPS), lax.rsqrt(s2 / HEAD_DIM + EPS))
        return x * r * g_ref[...]

    pos = lax.broadcasted_iota(jnp.int32, (s_len, 1), 0)
    hi = (pos & ~(ALIBI_SPLIT - 1)).astype(F32)
    lo = (pos & (ALIBI_SPLIT - 1)).astype(F32)
    one = jnp.ones((s_len, 1), F32)
    zero = jnp.zeros((s_len, 1), F32)
    q_aug = jnp.where(lane == HEAD_DIM, -slope * hi,
                      jnp.where(lane == HEAD_DIM + 1, -slope * lo,
                                jnp.where(lane <= HEAD_DIM + 3, one, zero)))
    k_aug = jnp.where(lane <= HEAD_DIM + 1, one,
                      jnp.where(lane == HEAD_DIM + 2, slope * hi,
                                jnp.where(lane == HEAD_DIM + 3, slope * lo, zero)))

    qn = qk_norm(q_ref, qg_ref) * (1.0 / math.sqrt(HEAD_DIM))
    kn = qk_norm(k_ref, kg_ref)
    qa_ref[0] = jnp.where(first, qn, q_aug).astype(BF16)
    qa_ref[1] = jnp.where(first, pltpu.roll(qn, HEAD_DIM, axis=1), q_aug).astype(BF16)
    ka_ref[0] = jnp.where(first, kn, k_aug).astype(BF16)
    ka_ref[1] = jnp.where(first, pltpu.roll(kn, HEAD_DIM, axis=1), k_aug).astype(BF16)

    row = lax.broadcasted_iota(jnp.int32, (tq, tq), 0)
    col = lax.broadcasted_iota(jnp.int32, (tq, tq), 1)
    causal = col <= row
    sg = sg_ref[...]

    for i in range(s_len // tq):
        kv = (i + 1) * tq

        def probs(c):
            s = lax.dot_general(qa_ref[c, i * tq:(i + 1) * tq, :], ka_ref[c, :kv, :], NT_DIMS,
                                preferred_element_type=F32)
            diag = jnp.where(causal, s[:, kv - tq:], -jnp.inf)
            s = diag if i == 0 else jnp.concatenate([s[:, :kv - tq], diag], axis=1)
            p = jnp.exp(s - jnp.max(s, axis=-1, keepdims=True))
            return p, jnp.sum(p, axis=-1, keepdims=True)

        p1, l1 = probs(0)
        p2, l2 = probs(1)
        a = p1 * (1.0 / l1) - p2 * (lam / l2)
        o = jnp.dot(a.astype(BF16), v_ref[:kv, :], preferred_element_type=F32)
        ms = jnp.mean(o * o, axis=-1, keepdims=True)
        o = o * lax.rsqrt(ms + EPS) * sg * (1.0 - lam_init)
        o_ref[i * tq:(i + 1) * tq, :] = o.astype(o_ref.dtype)


def _attention(proj, qg2, kg2, lq1, lk1, lq2, lk2, sg, *, batch, seq, n_heads, lam_init, tq):
    t = proj.shape[0]
    small = lambda w: pl.BlockSpec((1, w), lambda b, h: (0, 0))
    kern = functools.partial(_attn_kernel, tq=tq, n_heads=n_heads, lam_init=lam_init)
    return pl.pallas_call(
        kern,
        grid=(batch, n_heads),
        in_specs=[pl.BlockSpec((seq, V_DIM), lambda b, h: (b, h)),
                  pl.BlockSpec((seq, V_DIM), lambda b, h: (b, n_heads + h)),
                  pl.BlockSpec((seq, V_DIM), lambda b, h: (b, 2 * n_heads + h)),
                  small(V_DIM), small(V_DIM), small(HEAD_DIM), small(HEAD_DIM), small(HEAD_DIM),
                  small(HEAD_DIM), small(V_DIM)],
        out_specs=pl.BlockSpec((seq, V_DIM), lambda b, h: (b, h)),
        out_shape=jax.ShapeDtypeStruct((t, n_heads * V_DIM), BF16),
        scratch_shapes=[pltpu.VMEM((2, seq, V_DIM), BF16), pltpu.VMEM((2, seq, V_DIM), BF16)],
        compiler_params=_params("parallel", "parallel"),
        name="diff_attn",
    )(proj, proj, proj, qg2, kg2, lq1, lk1, lq2, lk2, sg)


PEER_CHUNKS = 4
DMA_QUEUES = 2
WAIT_GROUP = 8
SMEM_ROWS = 16
HALO = 16


def _mix_out_kernel(x_ref, a_ref, cb_ref, cc_ref, cx_ref, hc_ref, hx_ref, cw_ref, w_ref, o_ref, u_ref,
                    *, tiles_per_seq):
    tm = x_ref.shape[0]
    i = pl.program_id(0)
    u_ref[HALO:, :] = cc_ref[...].astype(F32) * cx_ref[...].astype(F32)
    halo = hc_ref[...].astype(F32) * hx_ref[...].astype(F32)
    u_ref[:HALO, :] = jnp.where(lax.rem(i, tiles_per_seq) == 0, 0.0, halo)
    cw = cw_ref[...]
    y = u_ref[HALO - 2:HALO - 2 + tm, :] * cw[0:1, :]
    y = y + u_ref[HALO - 1:HALO - 1 + tm, :] * cw[1:2, :]
    y = y + u_ref[HALO:, :] * cw[2:3, :]
    c = (cb_ref[...].astype(F32) * y).astype(BF16)
    aw = a_ref.shape[1]
    o = jnp.dot(a_ref[...], w_ref[:aw, :], preferred_element_type=F32)
    o = o + jnp.dot(c, w_ref[aw:, :], preferred_element_type=F32)
    o_ref[...] = x_ref[...] + o


def _mix_out(x2, a, proj, conv_w, w_out_bf, *, seq, tm, attn_w, conv_wd):
    t, d = x2.shape
    cblk = (3 * attn_w) // conv_wd
    hb = tm // HALO
    halo_map = lambda off: (lambda i: (jnp.maximum(i * hb - 1, 0), cblk + off))
    kern = functools.partial(_mix_out_kernel, tiles_per_seq=seq // tm)
    return pl.pallas_call(
        kern,
        grid=(t // tm,),
        in_specs=[pl.BlockSpec((tm, d), lambda i: (i, 0)),
                  pl.BlockSpec((tm, attn_w), lambda i: (i, 0)),
                  pl.BlockSpec((tm, conv_wd), lambda i: (i, cblk)),
                  pl.BlockSpec((tm, conv_wd), lambda i: (i, cblk + 1)),
                  pl.BlockSpec((tm, conv_wd), lambda i: (i, cblk + 2)),
                  pl.BlockSpec((HALO, conv_wd), halo_map(1)),
                  pl.BlockSpec((HALO, conv_wd), halo_map(2)),
                  pl.BlockSpec((CONV_K, conv_wd), lambda i: (0, 0)),
                  pl.BlockSpec((attn_w + conv_wd, d), lambda i: (0, 0))],
        out_specs=pl.BlockSpec((tm, d), lambda i: (i, 0)),
        out_shape=jax.ShapeDtypeStruct((t, d), F32),
        scratch_shapes=[pltpu.VMEM((tm + HALO, conv_wd), F32)],
        compiler_params=_params("parallel"),
        name="mix_out",
    )(x2, a, proj, proj, proj, proj, proj, conv_w, w_out_bf)


def _top_rows(work, k, ridx=None):
    n, tm = work.shape
    if ridx is None:
        ridx = lax.broadcasted_iota(jnp.int32, (n, tm), 0)
    krow = lax.broadcasted_iota(jnp.int32, (k, tm), 0)
    vals = jnp.zeros((k, tm), F32)
    idxs = jnp.zeros((k, tm), jnp.int32)
    for r in range(k):
        m = jnp.max(work, axis=0, keepdims=True)
        am = jnp.min(jnp.where(work == m, ridx, jnp.iinfo(jnp.int32).max), axis=0, keepdims=True)
        vals = jnp.where(krow == r, m, vals)
        idxs = jnp.where(krow == r, am, idxs)
        work = jnp.where(ridx == am, -jnp.inf, work)
    return vals, idxs


def _candidates(s1, s2, k):
    tm = s1.shape[1]
    sub = lax.broadcasted_iota(jnp.int32, (SUBLANES, tm), 0)
    root = math.isqrt(k)
    vals, idxs = [], []

    def add(v, flat, ok):
        vals.append(jnp.where(ok, v, -jnp.inf))
        idxs.append(jnp.where(ok, flat, k * k))

    for i in range(root):
        nj = k // (i + 1)
        for j0 in range(0, nj, SUBLANES):
            j = sub + j0
            add(s1[i:i + 1, :] + s2[j0:j0 + SUBLANES, :], i * k + j, j < nj)
    for j in range(k // (root + 1)):
        ni = k // (j + 1)
        for i0 in range(root // SUBLANES * SUBLANES, ni, SUBLANES):
            i = sub + i0
            add(s1[i0:i0 + SUBLANES, :] + s2[j:j + 1, :], i * k + j, (i >= root) & (i < ni))
    return jnp.concatenate(vals, axis=0), jnp.concatenate(idxs, axis=0)


def _select_rows(table, sel, k):
    out = jnp.zeros(sel.shape, table.dtype)
    for i in range(k):
        out = jnp.where(sel == i, table[i:i + 1, :], out)
    return out


def _route_kernel(x_ref, g_ref, wq_ref, sk_ref, h_ref, e_ref, gate_ref):
    x = x_ref[...]
    ms = jnp.mean(x * x, axis=-1, keepdims=True)
    h = x * lax.rsqrt(ms + EPS) * g_ref[...]
    h_ref[...] = h
    q = jnp.dot(h.astype(BF16), wq_ref[...], preferred_element_type=F32)
    k = PEER_TOPK
    experts, gates = [], []
    for hd in range(PEER_HEADS):
        sv, si = [], []
        for c in range(2):
            g = hd * 2 + c
            qg = q[:, g * PEER_HALF:(g + 1) * PEER_HALF].astype(BF16)
            sc = lax.dot_general(sk_ref[g], qg, NT_DIMS, preferred_element_type=F32)
            v, ix = _top_rows(sc, k)
            sv.append(v)
            si.append(ix)
        cand, flat = _candidates(sv[0], sv[1], k)
        top, idx = _top_rows(cand, k, flat)
        e1 = _select_rows(si[0], lax.shift_right_logical(idx, K_SHIFT), k)
        e2 = _select_rows(si[1], idx & (k - 1), k)
        experts.append(e1 * PEER_KEYS + e2)
        p = jnp.exp(top - top[0:1, :])
        gates.append(p / jnp.sum(p, axis=0, keepdims=True))
    e_ref[...] = jnp.concatenate(experts, axis=0).T
    gate_ref[...] = jnp.concatenate(gates, axis=0).T


def _route(x1, g, wq_bf, sk_bf, *, tm):
    t, d = x1.shape
    nq = wq_bf.shape[1]
    ng, nk, nh = sk_bf.shape
    ek = PEER_HEADS * PEER_TOPK
    return pl.pallas_call(
        _route_kernel,
        grid=(t // tm,),
        in_specs=[pl.BlockSpec((tm, d), lambda i: (i, 0)),
                  pl.BlockSpec((1, d), lambda i: (0, 0)),
                  pl.BlockSpec((d, nq), lambda i: (0, 0)),
                  pl.BlockSpec((ng, nk, nh), lambda i: (0, 0, 0))],
        out_specs=[pl.BlockSpec((tm, d), lambda i: (i, 0)),
                   pl.BlockSpec((tm, ek), lambda i: (i, 0)),
                   pl.BlockSpec((tm, ek), lambda i: (i, 0))],
        out_shape=[jax.ShapeDtypeStruct((t, d), F32),
                   jax.ShapeDtypeStruct((t, ek), jnp.int32),
                   jax.ShapeDtypeStruct((t, ek), F32)],
        compiler_params=_params("parallel"),
        name="peer_route",
    )(x1, g, wq_bf, sk_bf)


def _gelu(x):
    return 0.5 * x * (1.0 + lax.erf(x * (1.0 / math.sqrt(2.0))))


def _peer_kernel(e_ref, en_ref, g2_ref, h_ref, x_ref, tab_ref, o_ref, buf_ref, sem_ref, *, n_buf):
    tb, ek = e_ref.shape
    half = tab_ref.shape[2] // 2
    step = pl.program_id(0)
    last_step = pl.num_programs(0) - 1
    even = lax.broadcasted_iota(jnp.int32, (1, 2 * ek), 1) % 2 == 0

    cw = half // PEER_CHUNKS
    per = ek // PEER_CHUNKS

    def issue(idx_ref, t, slot, k0, k1):
        for k in range(k0, k1):
            pltpu.make_async_copy(tab_ref.at[idx_ref[t, k]], buf_ref.at[slot, pl.ds(k, 1)],
                                  sem_ref.at[slot]).start(priority=k % DMA_QUEUES)

    def wait(slot):
        pltpu.make_async_copy(tab_ref.at[pl.ds(0, ek), 0], buf_ref.at[slot], sem_ref.at[slot]).wait()

    def store(t, o):
        o_ref[pl.ds(t, 1), :half] = x_ref[pl.ds(t, 1), :half] + o[0:1, :]
        o_ref[pl.ds(t, 1), half:] = x_ref[pl.ds(t, 1), half:] + o[1:2, :]

    def s1_lhs(t):
        hrow = h_ref[pl.ds(t, 1), :].astype(BF16)
        return jnp.concatenate([hrow[:, :half], hrow[:, half:]], axis=0)

    def gate_act(r, t):
        s = r[0:1, :] + pltpu.roll(r[1:2, :], 2 * ek - 1, axis=1)
        act = jnp.where(even, _gelu(s) * g2_ref[pl.ds(t, 1), :], 0.0)
        return jnp.concatenate([act, pltpu.roll(act, 1, axis=1)], axis=0).astype(BF16)

    def body(t, slot, carry, *, s1=True, act=True, s2=True, row_out=None, nxt=None):
        r_prev, a2_prev, o_prev = carry
        if s1:
            if slot % WAIT_GROUP == 0:
                for w in range(WAIT_GROUP):
                    wait(slot + w)
            hh = s1_lhs(t)
        a2 = gate_act(r_prev, t - 1) if act else a2_prev
        slot_v = (slot - 2) % n_buf
        r = jnp.zeros((2, 2 * ek), F32)
        outs = []
        for c in range(PEER_CHUNKS):
            if s1:
                rows = pltpu.bitcast(buf_ref[slot, :, c * cw:(c + 1) * cw], BF16)
                r = r + lax.dot_general(hh[:, c * cw:(c + 1) * cw], rows, NT_DIMS,
                                        preferred_element_type=F32)
            if s2:
                rows = pltpu.bitcast(buf_ref[slot_v, :, half + c * cw:half + (c + 1) * cw], BF16)
                outs.append(jnp.dot(a2_prev, rows, preferred_element_type=F32))
            if nxt is not None:
                issue(*nxt, c * per, (c + 1) * per)
        if row_out is not None:
            store(row_out, o_prev)
        return r, a2, (jnp.concatenate(outs, axis=1) if s2 else o_prev)

    look = n_buf - 3

    @pl.when(step == 0)
    def _():
        for j in range(look):
            issue(e_ref, j, j, 0, ek)

    def future(t):
        tn = t + look
        return (e_ref, tn, tn % n_buf) if tn < tb else (en_ref, tn - tb, tn % n_buf)

    carry = (jnp.zeros((2, 2 * ek), F32), jnp.zeros((2, 2 * ek), BF16), jnp.zeros((2, half), F32))
    carry = body(0, 0, carry, act=False, s2=False, nxt=future(0))
    carry = body(1, 1, carry, s2=False, nxt=future(1))

    def group(g, carry):
        for j in range(n_buf):
            t = 2 + g * n_buf + j
            slot = (2 + j) % n_buf
            row_out = t - 3 if j else jnp.maximum(t - 3, 0)
            carry = body(t, slot, carry, row_out=row_out, nxt=(e_ref, t + look, (slot + look) % n_buf))
        return carry

    n_groups = (tb - 2) // n_buf
    assert n_groups * n_buf + 1 + look < tb
    carry = lax.fori_loop(0, n_groups, group, carry)
    for t in range(2 + n_groups * n_buf, tb):
        carry = body(t, t % n_buf, carry, row_out=t - 3, nxt=future(t))
    carry = body(tb, tb % n_buf, carry, s1=False, row_out=tb - 3)
    carry = body(tb + 1, (tb + 1) % n_buf, carry, s1=False, act=False, row_out=tb - 2)
    store(tb - 1, carry[2])

    @pl.when(step == last_step)
    def _():
        for j in range(look):
            wait(j)


def _peer(experts, gates2, h, x1, table, *, tb, n_buf):
    t, d = x1.shape
    ek = experts.shape[1]
    n_steps = t // tb
    assert tb % n_buf == 0 and n_buf - 3 <= SMEM_ROWS and tb % SMEM_ROWS == 0 and n_buf % WAIT_GROUP == 0
    nb = tb // SMEM_ROWS
    kern = functools.partial(_peer_kernel, n_buf=n_buf)
    return pl.pallas_call(
        kern,
        grid=(n_steps,),
        in_specs=[pl.BlockSpec((tb, ek), lambda i: (i, 0), memory_space=pltpu.SMEM),
                  pl.BlockSpec((SMEM_ROWS, ek), lambda i: (jnp.minimum(i + 1, n_steps - 1) * nb, 0),
                               memory_space=pltpu.SMEM),
                  pl.BlockSpec((tb, 2 * ek), lambda i: (i, 0)),
                  pl.BlockSpec((tb, d), lambda i: (i, 0)),
                  pl.BlockSpec((tb, d), lambda i: (i, 0)),
                  pl.BlockSpec(memory_space=pl.ANY)],
        out_specs=pl.BlockSpec((tb, d), lambda i: (i, 0)),
        out_shape=jax.ShapeDtypeStruct((t, d), F32),
        scratch_shapes=[pltpu.VMEM((n_buf, ek, table.shape[2]), jnp.uint32),
                        pltpu.SemaphoreType.DMA((n_buf,))],
        compiler_params=_params("arbitrary"),
        name="peer_experts",
    )(experts, experts, gates2, h, x1, table)


def _pack_table(u, v):
    half = u.shape[1] // 2

    def pack(w):
        bits = lax.bitcast_convert_type(w.astype(BF16), jnp.uint16).astype(jnp.uint32)
        return bits[:, :half] | (bits[:, half:] << 16)

    return jnp.concatenate([pack(u), pack(v)], axis=1)[:, None, :]


def _layer(x2, l, attn_norm_g, w_in, q_norm_g, k_norm_g, lambda_q1, lambda_k1, lambda_q2, lambda_k2,
           subln_g, conv_w, w_out, ffn_norm_g, peer_w_q, peer_subkeys, peer_u, peer_v, *, batch, seq):
    t, d = x2.shape
    attn_w = d // 2
    conv_wd = d - attn_w
    n_heads = attn_w // V_DIM
    lam_init = 0.8 - 0.6 * math.exp(-0.3 * l)
    row = lambda p: p[l].reshape(1, -1).astype(F32)
    row2 = lambda p: jnp.tile(p[l].reshape(1, -1).astype(F32), (1, 2))

    proj = _in_proj(x2, row(attn_norm_g), w_in[l].astype(BF16), tm=512, tn=1536)
    a = _attention(proj, row2(q_norm_g), row2(k_norm_g), row(lambda_q1), row(lambda_k1), row(lambda_q2),
                   row(lambda_k2), row(subln_g), batch=batch, seq=seq, n_heads=n_heads,
                   lam_init=lam_init, tq=256)
    x1 = _mix_out(x2, a, proj, conv_w[l].astype(F32), w_out[l].astype(BF16), seq=seq, tm=256,
                  attn_w=attn_w, conv_wd=conv_wd)
    sk = peer_subkeys[l].reshape(PEER_HEADS * 2, PEER_KEYS, PEER_HALF).astype(BF16)
    h, experts, gates = _route(x1, row(ffn_norm_g), peer_w_q[l].astype(BF16), sk, tm=256)
    gates2 = jnp.stack([gates, jnp.zeros_like(gates)], axis=-1).reshape(t, -1)
    table = _pack_table(peer_u[l], peer_v[l])
    return _peer(experts, gates2, h, x1, table, tb=64, n_buf=16)


def kernel(x, attn_norm_g, w_in, q_norm_g, k_norm_g, lambda_q1, lambda_k1, lambda_q2, lambda_k2, subln_g,
           conv_w, w_out, ffn_norm_g, peer_w_q, peer_subkeys, peer_u, peer_v):
    b, s, d = x.shape
    x2 = x.reshape(b * s, d)
    for l in range(w_in.shape[0]):
        x2 = _layer(x2, l, attn_norm_g, w_in, q_norm_g, k_norm_g, lambda_q1, lambda_k1, lambda_q2,
                    lambda_k2, subln_g, conv_w, w_out, ffn_norm_g, peer_w_q, peer_subkeys, peer_u, peer_v,
                    batch=b, seq=s)
    return x2.reshape(b, s, d)
```

```python
import functools
import math

import jax
import jax.numpy as jnp
from jax import lax
from jax.experimental import pallas as pl
from jax.experimental.pallas import tpu as pltpu

EPS = 1e-6
HEAD_DIM = 64
V_DIM = 2 * HEAD_DIM
CONV_K = 3
PEER_HEADS = 8
PEER_KEYS = 128
PEER_HALF = 128
PEER_TOPK = 16
K_SHIFT = 4
LANES = 128
SUBLANES = 8
ALIBI_SPLIT = 256
VMEM_LIMIT = 56 * 1024 * 1024

F32 = jnp.float32
BF16 = jnp.bfloat16
NT_DIMS = (((1,), (1,)), ((), ()))


def _params(*sem):
    return pltpu.CompilerParams(dimension_semantics=sem, vmem_limit_bytes=VMEM_LIMIT)


def _in_proj_kernel(x_ref, g_ref, w_ref, o_ref, h_ref):
    @pl.when(pl.program_id(1) == 0)
    def _():
        x = x_ref[...]
        ms = jnp.mean(x * x, axis=-1, keepdims=True)
        h_ref[...] = (x * lax.rsqrt(ms + EPS) * g_ref[...]).astype(BF16)

    o_ref[...] = jnp.dot(h_ref[...], w_ref[...], preferred_element_type=F32).astype(o_ref.dtype)


def _in_proj(x2, g, w_bf, *, tm, tn):
    t, d = x2.shape
    n = w_bf.shape[1]
    return pl.pallas_call(
        _in_proj_kernel,
        grid=(t // tm, n // tn),
        in_specs=[pl.BlockSpec((tm, d), lambda i, j: (i, 0)),
                  pl.BlockSpec((1, d), lambda i, j: (0, 0)),
                  pl.BlockSpec((d, tn), lambda i, j: (0, j))],
        out_specs=pl.BlockSpec((tm, tn), lambda i, j: (i, j)),
        out_shape=jax.ShapeDtypeStruct((t, n), BF16),
        scratch_shapes=[pltpu.VMEM((tm, d), BF16)],
        compiler_params=_params("parallel", "arbitrary"),
        name="in_proj",
    )(x2, g, w_bf)


def _attn_kernel(q_ref, k_ref, v_ref, qg_ref, kg_ref, lq1_ref, lk1_ref, lq2_ref, lk2_ref, sg_ref,
                 o_ref, qa_ref, ka_ref, *, tq, n_heads, lam_init):
    s_len = q_ref.shape[0]
    lane = lax.broadcasted_iota(jnp.int32, (1, LANES), 1)
    first = lane < HEAD_DIM

    lam = (jnp.exp(jnp.sum(lq1_ref[...] * lk1_ref[...], axis=-1, keepdims=True))
           - jnp.exp(jnp.sum(lq2_ref[...] * lk2_ref[...], axis=-1, keepdims=True)) + lam_init)
    head = (pl.program_id(1) + 1).astype(F32)
    slope = jnp.exp2(jnp.full((1, 1), -8.0 / n_heads, F32) * head)

    def qk_norm(x_ref, g_ref):
        x = x_ref[...].astype(F32)
        x2 = x * x
        s1 = jnp.sum(jnp.where(first, x2, 0.0), axis=-1, keepdims=True)
        s2 = jnp.sum(jnp.where(first, 0.0, x2), axis=-1, keepdims=True)
        r = jnp.where(first, lax.rsqrt(s1 / HEAD_DIM + EPS), lax.rsqrt(s2 / HEAD_DIM + EPS))
        return x * r * g_ref[...]

    pos = lax.broadcasted_iota(jnp.int32, (s_len, 1), 0)
    hi = (pos & ~(ALIBI_SPLIT - 1)).astype(F32)
    lo = (pos & (ALIBI_SPLIT - 1)).astype(F32)
    one = jnp.ones((s_len, 1), F32)
    zero = jnp.zeros((s_len, 1), F32)
    q_aug = jnp.where(lane == HEAD_DIM, -slope * hi,
                      jnp.where(lane == HEAD_DIM + 1, -slope * lo,
                                jnp.where(lane <= HEAD_DIM + 3, one, zero)))
    k_aug = jnp.where(lane <= HEAD_DIM + 1, one,
                      jnp.where(lane == HEAD_DIM + 2, slope * hi,
                                jnp.where(lane == HEAD_DIM + 3, slope * lo, zero)))

    qn = qk_norm(q_ref, qg_ref) * (1.0 / math.sqrt(HEAD_DIM))
    kn = qk_norm(k_ref, kg_ref)
    qa_ref[0] = jnp.where(first, qn, q_aug).astype(BF16)
    qa_ref[1] = jnp.where(first, pltpu.roll(qn, HEAD_DIM, axis=1), q_aug).astype(BF16)
    ka_ref[0] = jnp.where(first, kn, k_aug).astype(BF16)
    ka_ref[1] = jnp.where(first, pltpu.roll(kn, HEAD_DIM, axis=1), k_aug).astype(BF16)

    row = lax.broadcasted_iota(jnp.int32, (tq, tq), 0)
    col = lax.broadcasted_iota(jnp.int32, (tq, tq), 1)
    causal = col <= row
    sg = sg_ref[...]

    for i in range(s_len // tq):
        kv = (i + 1) * tq

        def probs(c):
            s = lax.dot_general(qa_ref[c, i * tq:(i + 1) * tq, :], ka_ref[c, :kv, :], NT_DIMS,
                                preferred_element_type=F32)
            diag = jnp.where(causal, s[:, kv - tq:], -jnp.inf)
            s = diag if i == 0 else jnp.concatenate([s[:, :kv - tq], diag], axis=1)
            p = jnp.exp(s - jnp.max(s, axis=-1, keepdims=True))
            return p, jnp.sum(p, axis=-1, keepdims=True)

        p1, l1 = probs(0)
        p2, l2 = probs(1)
        a = p1 * (1.0 / l1) - p2 * (lam / l2)
        o = jnp.dot(a.astype(BF16), v_ref[:kv, :], preferred_element_type=F32)
        ms = jnp.mean(o * o, axis=-1, keepdims=True)
        o = o * lax.rsqrt(ms + EPS) * sg * (1.0 - lam_init)
        o_ref[i * tq:(i + 1) * tq, :] = o.astype(o_ref.dtype)


def _attention(proj, qg2, kg2, lq1, lk1, lq2, lk2, sg, *, batch, seq, n_heads, lam_init, tq):
    t = proj.shape[0]
    small = lambda w: pl.BlockSpec((1, w), lambda b, h: (0, 0))
    kern = functools.partial(_attn_kernel, tq=tq, n_heads=n_heads, lam_init=lam_init)
    return pl.pallas_call(
        kern,
        grid=(batch, n_heads),
        in_specs=[pl.BlockSpec((seq, V_DIM), lambda b, h: (b, h)),
                  pl.BlockSpec((seq, V_DIM), lambda b, h: (b, n_heads + h)),
                  pl.BlockSpec((seq, V_DIM), lambda b, h: (b, 2 * n_heads + h)),
                  small(V_DIM), small(V_DIM), small(HEAD_DIM), small(HEAD_DIM), small(HEAD_DIM),
                  small(HEAD_DIM), small(V_DIM)],
        out_specs=pl.BlockSpec((seq, V_DIM), lambda b, h: (b, h)),
        out_shape=jax.ShapeDtypeStruct((t, n_heads * V_DIM), BF16),
        scratch_shapes=[pltpu.VMEM((2, seq, V_DIM), BF16), pltpu.VMEM((2, seq, V_DIM), BF16)],
        compiler_params=_params("parallel", "parallel"),
        name="diff_attn",
    )(proj, proj, proj, qg2, kg2, lq1, lk1, lq2, lk2, sg)


PEER_CHUNKS = 4
DMA_QUEUES = 2
WAIT_GROUP = 8
SMEM_ROWS = 16
HALO = 16


def _mix_out_kernel(x_ref, a_ref, cb_ref, cc_ref, cx_ref, hc_ref, hx_ref, cw_ref, w_ref, o_ref, u_ref,
                    *, tiles_per_seq):
    tm = x_ref.shape[0]
    i = pl.program_id(0)
    u_ref[HALO:, :] = cc_ref[...].astype(F32) * cx_ref[...].astype(F32)
    halo = hc_ref[...].astype(F32) * hx_ref[...].astype(F32)
    u_ref[:HALO, :] = jnp.where(lax.rem(i, tiles_per_seq) == 0, 0.0, halo)
    cw = cw_ref[...]
    y = u_ref[HALO - 2:HALO - 2 + tm, :] * cw[0:1, :]
    y = y + u_ref[HALO - 1:HALO - 1 + tm, :] * cw[1:2, :]
    y = y + u_ref[HALO:, :] * cw[2:3, :]
    c = (cb_ref[...].astype(F32) * y).astype(BF16)
    aw = a_ref.shape[1]
    o = jnp.dot(a_ref[...], w_ref[:aw, :], preferred_element_type=F32)
    o = o + jnp.dot(c, w_ref[aw:, :], preferred_element_type=F32)
    o_ref[...] = x_ref[...] + o


def _mix_out(x2, a, proj, conv_w, w_out_bf, *, seq, tm, attn_w, conv_wd):
    t, d = x2.shape
    cblk = (3 * attn_w) // conv_wd
    hb = tm // HALO
    halo_map = lambda off: (lambda i: (jnp.maximum(i * hb - 1, 0), cblk + off))
    kern = functools.partial(_mix_out_kernel, tiles_per_seq=seq // tm)
    return pl.pallas_call(
        kern,
        grid=(t // tm,),
        in_specs=[pl.BlockSpec((tm, d), lambda i: (i, 0)),
                  pl.BlockSpec((tm, attn_w), lambda i: (i, 0)),
                  pl.BlockSpec((tm, conv_wd), lambda i: (i, cblk)),
                  pl.BlockSpec((tm, conv_wd), lambda i: (i, cblk + 1)),
                  pl.BlockSpec((tm, conv_wd), lambda i: (i, cblk + 2)),
                  pl.BlockSpec((HALO, conv_wd), halo_map(1)),
                  pl.BlockSpec((HALO, conv_wd), halo_map(2)),
                  pl.BlockSpec((CONV_K, conv_wd), lambda i: (0, 0)),
                  pl.BlockSpec((attn_w + conv_wd, d), lambda i: (0, 0))],
        out_specs=pl.BlockSpec((tm, d), lambda i: (i, 0)),
        out_shape=jax.ShapeDtypeStruct((t, d), F32),
        scratch_shapes=[pltpu.VMEM((tm + HALO, conv_wd), F32)],
        compiler_params=_params("parallel"),
        name="mix_out",
    )(x2, a, proj, proj, proj, proj, proj, conv_w, w_out_bf)


def _top_rows(work, k, ridx=None):
    n, tm = work.shape
    if ridx is None:
        ridx = lax.broadcasted_iota(jnp.int32, (n, tm), 0)
    krow = lax.broadcasted_iota(jnp.int32, (k, tm), 0)
    vals = jnp.zeros((k, tm), F32)
    idxs = jnp.zeros((k, tm), jnp.int32)
    for r in range(k):
        m = jnp.max(work, axis=0, keepdims=True)
        am = jnp.min(jnp.where(work == m, ridx, jnp.iinfo(jnp.int32).max), axis=0, keepdims=True)
        vals = jnp.where(krow == r, m, vals)
        idxs = jnp.where(krow == r, am, idxs)
        work = jnp.where(ridx == am, -jnp.inf, work)
    return vals, idxs


def _candidates(s1, s2, k):
    tm = s1.shape[1]
    sub = lax.broadcasted_iota(jnp.int32, (SUBLANES, tm), 0)
    root = math.isqrt(k)
    vals, idxs = [], []

    def add(v, flat, ok):
        vals.append(jnp.where(ok, v, -jnp.inf))
        idxs.append(jnp.where(ok, flat, k * k))

    for i in range(root):
        nj = k // (i + 1)
        for j0 in range(0, nj, SUBLANES):
            j = sub + j0
            add(s1[i:i + 1, :] + s2[j0:j0 + SUBLANES, :], i * k + j, j < nj)
    for j in range(k // (root + 1)):
        ni = k // (j + 1)
        for i0 in range(root // SUBLANES * SUBLANES, ni, SUBLANES):
            i = sub + i0
            add(s1[i0:i0 + SUBLANES, :] + s2[j:j + 1, :], i * k + j, (i >= root) & (i < ni))
    return jnp.concatenate(vals, axis=0), jnp.concatenate(idxs, axis=0)


def _select_rows(table, sel, k):
    out = jnp.zeros(sel.shape, table.dtype)
    for i in range(k):
        out = jnp.where(sel == i, table[i:i + 1, :], out)
    return out


def _route_kernel(x_ref, g_ref, wq_ref, sk_ref, h_ref, e_ref, gate_ref):
    x = x_ref[...]
    ms = jnp.mean(x * x, axis=-1, keepdims=True)
    h = x * lax.rsqrt(ms + EPS) * g_ref[...]
    h_ref[...] = h
    q = jnp.dot(h.astype(BF16), wq_ref[...], preferred_element_type=F32)
    k = PEER_TOPK
    experts, gates = [], []
    for hd in range(PEER_HEADS):
        sv, si = [], []
        for c in range(2):
            g = hd * 2 + c
            qg = q[:, g * PEER_HALF:(g + 1) * PEER_HALF].astype(BF16)
            sc = lax.dot_general(sk_ref[g], qg, NT_DIMS, preferred_element_type=F32)
            v, ix = _top_rows(sc, k)
            sv.append(v)
            si.append(ix)
        cand, flat = _candidates(sv[0], sv[1], k)
        top, idx = _top_rows(cand, k, flat)
        e1 = _select_rows(si[0], lax.shift_right_logical(idx, K_SHIFT), k)
        e2 = _select_rows(si[1], idx & (k - 1), k)
        experts.append(e1 * PEER_KEYS + e2)
        p = jnp.exp(top - top[0:1, :])
        gates.append(p / jnp.sum(p, axis=0, keepdims=True))
    e_ref[...] = jnp.concatenate(experts, axis=0).T
    gate_ref[...] = jnp.concatenate(gates, axis=0).T


def _route(x1, g, wq_bf, sk_bf, *, tm):
    t, d = x1.shape
    nq = wq_bf.shape[1]
    ng, nk, nh = sk_bf.shape
    ek = PEER_HEADS * PEER_TOPK
    return pl.pallas_call(
        _route_kernel,
        grid=(t // tm,),
        in_specs=[pl.BlockSpec((tm, d), lambda i: (i, 0)),
                  pl.BlockSpec((1, d), lambda i: (0, 0)),
                  pl.BlockSpec((d, nq), lambda i: (0, 0)),
                  pl.BlockSpec((ng, nk, nh), lambda i: (0, 0, 0))],
        out_specs=[pl.BlockSpec((tm, d), lambda i: (i, 0)),
                   pl.BlockSpec((tm, ek), lambda i: (i, 0)),
                   pl.BlockSpec((tm, ek), lambda i: (i, 0))],
        out_shape=[jax.ShapeDtypeStruct((t, d), F32),
                   jax.ShapeDtypeStruct((t, ek), jnp.int32),
                   jax.ShapeDtypeStruct((t, ek), F32)],
        compiler_params=_params("parallel"),
        name="peer_route",
    )(x1, g, wq_bf, sk_bf)


def _gelu(x):
    return 0.5 * x * (1.0 + lax.erf(x * (1.0 / math.sqrt(2.0))))


def _peer_kernel(e_ref, en_ref, g2_ref, h_ref, x_ref, tab_ref, o_ref, buf_ref, sem_ref, *, n_buf):
    tb, ek = e_ref.shape
    half = tab_ref.shape[2] // 2
    step = pl.program_id(0)
    last_step = pl.num_programs(0) - 1
    even = lax.broadcasted_iota(jnp.int32, (1, 2 * ek), 1) % 2 == 0

    cw = half // PEER_CHUNKS
    per = ek // PEER_CHUNKS
    look = n_buf - 3

    def issue(idx_ref, t, slot, k0, k1):
        for k in range(k0, k1):
            pltpu.make_async_copy(tab_ref.at[idx_ref[t, k]], buf_ref.at[slot, pl.ds(2 * k, 2)],
                                  sem_ref.at[slot]).start(priority=k % DMA_QUEUES)

    def wait(slot):
        pltpu.make_async_copy(buf_ref.at[slot], buf_ref.at[slot], sem_ref.at[slot]).wait()

    def store(t, o):
        o_ref[pl.ds(t, 1), :half] = x_ref[pl.ds(t, 1), :half] + o[0:1, :]
        o_ref[pl.ds(t, 1), half:] = x_ref[pl.ds(t, 1), half:] + o[1:2, :]

    def s1_lhs(t):
        hrow = h_ref[pl.ds(t, 1), :].astype(BF16)
        return jnp.concatenate([hrow[:, :half], hrow[:, half:]], axis=0)

    def gate_act(r, t):
        s = r[0:1, :] + pltpu.roll(r[1:2, :], 2 * ek - 1, axis=1)
        act = jnp.where(even, _gelu(s) * g2_ref[pl.ds(t, 1), :], 0.0)
        return jnp.concatenate([act, pltpu.roll(act, 1, axis=1)], axis=0).astype(BF16)

    def body(t, slot, carry, *, s1=True, act=True, s2=True, row_out=None, nxt=None):
        r_prev, a2_prev, o_prev = carry
        if s1:
            if slot % WAIT_GROUP == 0:
                for w in range(WAIT_GROUP):
                    wait(slot + w)
            hh = s1_lhs(t)
        a2 = gate_act(r_prev, t - 1) if act else a2_prev
        slot_v = (slot - 2) % n_buf
        r = jnp.zeros((2, 2 * ek), F32)
        outs = []
        for c in range(PEER_CHUNKS):
            if s1:
                rows = buf_ref[slot, :, c * cw:(c + 1) * cw]
                r = r + lax.dot_general(hh[:, c * cw:(c + 1) * cw], rows, NT_DIMS,
                                        preferred_element_type=F32)
            if s2:
                rows = buf_ref[slot_v, :, half + c * cw:half + (c + 1) * cw]
                outs.append(jnp.dot(a2_prev, rows, preferred_element_type=F32))
            if nxt is not None:
                issue(*nxt, c * per, (c + 1) * per)
        if row_out is not None:
            store(row_out, o_prev)
        return r, a2, (jnp.concatenate(outs, axis=1) if s2 else o_prev)

    @pl.when(step == 0)
    def _():
        for j in range(look):
            issue(e_ref, j, j, 0, ek)

    def future(t):
        tn = t + look
        return (e_ref, tn, tn % n_buf) if tn < tb else (en_ref, tn - tb, tn % n_buf)

    carry = (jnp.zeros((2, 2 * ek), F32), jnp.zeros((2, 2 * ek), BF16), jnp.zeros((2, half), F32))
    carry = body(0, 0, carry, act=False, s2=False, nxt=future(0))
    carry = body(1, 1, carry, s2=False, nxt=future(1))

    def group(g, carry):
        for j in range(n_buf):
            t = 2 + g * n_buf + j
            slot = (2 + j) % n_buf
            row_out = t - 3 if j else jnp.maximum(t - 3, 0)
            carry = body(t, slot, carry, row_out=row_out, nxt=(e_ref, t + look, (slot + look) % n_buf))
        return carry

    n_groups = (tb - 2 - look) // n_buf
    carry = lax.fori_loop(0, n_groups, group, carry)
    for t in range(2 + n_groups * n_buf, tb):
        carry = body(t, t % n_buf, carry, row_out=t - 3, nxt=future(t))
    carry = body(tb, tb % n_buf, carry, s1=False, row_out=tb - 3)
    carry = body(tb + 1, (tb + 1) % n_buf, carry, s1=False, act=False, row_out=tb - 2)
    store(tb - 1, carry[2])

    @pl.when(step == last_step)
    def _():
        for j in range(look):
            wait(j)


def _peer(experts, gates2, h, x1, table, *, tb, n_buf):
    t, d = x1.shape
    ek = experts.shape[1]
    n_steps = t // tb
    assert tb % n_buf == 0 and n_buf - 3 <= SMEM_ROWS and tb % SMEM_ROWS == 0 and n_buf % WAIT_GROUP == 0
    assert n_buf & (n_buf - 1) == 0
    nb = tb // SMEM_ROWS
    kern = functools.partial(_peer_kernel, n_buf=n_buf)
    return pl.pallas_call(
        kern,
        grid=(n_steps,),
        in_specs=[pl.BlockSpec((tb, ek), lambda i: (i, 0), memory_space=pltpu.SMEM),
                  pl.BlockSpec((SMEM_ROWS, ek), lambda i: (jnp.minimum(i + 1, n_steps - 1) * nb, 0),
                               memory_space=pltpu.SMEM),
                  pl.BlockSpec((tb, 2 * ek), lambda i: (i, 0)),
                  pl.BlockSpec((tb, d), lambda i: (i, 0)),
                  pl.BlockSpec((tb, d), lambda i: (i, 0)),
                  pl.BlockSpec(memory_space=pl.ANY)],
        out_specs=pl.BlockSpec((tb, d), lambda i: (i, 0)),
        out_shape=jax.ShapeDtypeStruct((t, d), F32),
        scratch_shapes=[pltpu.VMEM((n_buf, 2 * ek, table.shape[2]), BF16),
                        pltpu.SemaphoreType.DMA((n_buf,))],
        compiler_params=_params("arbitrary"),
        name="peer_experts",
    )(experts, experts, gates2, h, x1, table)


def _pack_kernel(u_ref, v_ref, lo_ref, hi_ref):
    half = u_ref.shape[1] // 2
    lo_ref[:, :half] = u_ref[:, :half].astype(BF16)
    lo_ref[:, half:] = v_ref[:, :half].astype(BF16)
    hi_ref[:, :half] = u_ref[:, half:].astype(BF16)
    hi_ref[:, half:] = v_ref[:, half:].astype(BF16)


def _pack_table(u, v, *, te):
    e, d = u.shape
    spec = pl.BlockSpec((te, d), lambda i: (i, 0))
    lo, hi = pl.pallas_call(
        _pack_kernel,
        grid=(e // te,),
        in_specs=[spec, spec],
        out_specs=[spec, spec],
        out_shape=[jax.ShapeDtypeStruct((e, d), BF16)] * 2,
        compiler_params=_params("parallel"),
        name="peer_pack",
    )(u, v)
    return jnp.stack([lo, hi], axis=1)


def _layer(x2, l, attn_norm_g, w_in, q_norm_g, k_norm_g, lambda_q1, lambda_k1, lambda_q2, lambda_k2,
           subln_g, conv_w, w_out, ffn_norm_g, peer_w_q, peer_subkeys, peer_u, peer_v, *, batch, seq):
    t, d = x2.shape
    attn_w = d // 2
    conv_wd = d - attn_w
    n_heads = attn_w // V_DIM
    lam_init = 0.8 - 0.6 * math.exp(-0.3 * l)
    row = lambda p: p[l].reshape(1, -1).astype(F32)
    row2 = lambda p: jnp.tile(p[l].reshape(1, -1).astype(F32), (1, 2))

    proj = _in_proj(x2, row(attn_norm_g), w_in[l].astype(BF16), tm=512, tn=1536)
    a = _attention(proj, row2(q_norm_g), row2(k_norm_g), row(lambda_q1), row(lambda_k1), row(lambda_q2),
                   row(lambda_k2), row(subln_g), batch=batch, seq=seq, n_heads=n_heads,
                   lam_init=lam_init, tq=256)
    x1 = _mix_out(x2, a, proj, conv_w[l].astype(F32), w_out[l].astype(BF16), seq=seq, tm=256,
                  attn_w=attn_w, conv_wd=conv_wd)
    sk = peer_subkeys[l].reshape(PEER_HEADS * 2, PEER_KEYS, PEER_HALF).astype(BF16)
    h, experts, gates = _route(x1, row(ffn_norm_g), peer_w_q[l].astype(BF16), sk, tm=256)
    gates2 = jnp.stack([gates, jnp.zeros_like(gates)], axis=-1).reshape(t, -1)
    table = _pack_table(peer_u[l], peer_v[l], te=512)
    return _peer(experts, gates2, h, x1, table, tb=128, n_buf=16)


def kernel(x, attn_norm_g, w_in, q_norm_g, k_norm_g, lambda_q1, lambda_k1, lambda_q2, lambda_k2, subln_g,
           conv_w, w_out, ffn_norm_g, peer_w_q, peer_subkeys, peer_u, peer_v):
    b, s, d = x.shape
    x2 = x.reshape(b * s, d)
    for l in range(w_in.shape[0]):
        x2 = _layer(x2, l, attn_norm_g, w_in, q_norm_g, k_norm_g, lambda_q1, lambda_k1, lambda_q2,
                    lambda_k2, subln_g, conv_w, w_out, ffn_norm_g, peer_w_q, peer_subkeys, peer_u, peer_v,
                    batch=b, seq=s)
    return x2.reshape(b, s, d)
```

```python
import functools
import math

import jax
import jax.numpy as jnp
from jax import lax
from jax.experimental import pallas as pl
from jax.experimental.pallas import tpu as pltpu

EPS = 1e-6
HEAD_DIM = 64
V_DIM = 2 * HEAD_DIM
CONV_K = 3
PEER_HEADS = 8
PEER_KEYS = 128
PEER_HALF = 128
PEER_TOPK = 16
K_SHIFT = 4
LANES = 128
SUBLANES = 8
ALIBI_SPLIT = 256
VMEM_LIMIT = 56 * 1024 * 1024

F32 = jnp.float32
BF16 = jnp.bfloat16
NT_DIMS = (((1,), (1,)), ((), ()))


def _params(*sem):
    return pltpu.CompilerParams(dimension_semantics=sem, vmem_limit_bytes=VMEM_LIMIT)


def _in_proj_kernel(x_ref, g_ref, w_ref, o_ref, h_ref):
    @pl.when(pl.program_id(1) == 0)
    def _():
        x = x_ref[...]
        ms = jnp.mean(x * x, axis=-1, keepdims=True)
        h_ref[...] = (x * lax.rsqrt(ms + EPS) * g_ref[...]).astype(BF16)

    o_ref[...] = jnp.dot(h_ref[...], w_ref[...], preferred_element_type=F32).astype(o_ref.dtype)


def _in_proj(x2, g, w_bf, *, tm, tn):
    t, d = x2.shape
    n = w_bf.shape[1]
    return pl.pallas_call(
        _in_proj_kernel,
        grid=(t // tm, n // tn),
        in_specs=[pl.BlockSpec((tm, d), lambda i, j: (i, 0)),
                  pl.BlockSpec((1, d), lambda i, j: (0, 0)),
                  pl.BlockSpec((d, tn), lambda i, j: (0, j))],
        out_specs=pl.BlockSpec((tm, tn), lambda i, j: (i, j)),
        out_shape=jax.ShapeDtypeStruct((t, n), BF16),
        scratch_shapes=[pltpu.VMEM((tm, d), BF16)],
        compiler_params=_params("parallel", "arbitrary"),
        name="in_proj",
    )(x2, g, w_bf)


def _attn_kernel(q_ref, k_ref, v_ref, qg_ref, kg_ref, lq1_ref, lk1_ref, lq2_ref, lk2_ref, sg_ref,
                 o_ref, qa_ref, ka_ref, *, tq, n_heads, lam_init):
    s_len = q_ref.shape[0]
    lane = lax.broadcasted_iota(jnp.int32, (1, LANES), 1)
    first = lane < HEAD_DIM

    lam = (jnp.exp(jnp.sum(lq1_ref[...] * lk1_ref[...], axis=-1, keepdims=True))
           - jnp.exp(jnp.sum(lq2_ref[...] * lk2_ref[...], axis=-1, keepdims=True)) + lam_init)
    head = (pl.program_id(1) + 1).astype(F32)
    slope = jnp.exp2(jnp.full((1, 1), -8.0 / n_heads, F32) * head)

    def qk_norm(x_ref, g_ref):
        x = x_ref[...].astype(F32)
        x2 = x * x
        s1 = jnp.sum(jnp.where(first, x2, 0.0), axis=-1, keepdims=True)
        s2 = jnp.sum(jnp.where(first, 0.0, x2), axis=-1, keepdims=True)
        r = jnp.where(first, lax.rsqrt(s1 / HEAD_DIM + EPS), lax.rsqrt(s2 / HEAD_DIM + EPS))
        return x * r * g_ref[...]

    pos = lax.broadcasted_iota(jnp.int32, (s_len, 1), 0)
    hi = (pos & ~(ALIBI_SPLIT - 1)).astype(F32)
    lo = (pos & (ALIBI_SPLIT - 1)).astype(F32)
    one = jnp.ones((s_len, 1), F32)
    zero = jnp.zeros((s_len, 1), F32)
    q_aug = jnp.where(lane == HEAD_DIM, -slope * hi,
                      jnp.where(lane == HEAD_DIM + 1, -slope * lo,
                                jnp.where(lane <= HEAD_DIM + 3, one, zero)))
    k_aug = jnp.where(lane <= HEAD_DIM + 1, one,
                      jnp.where(lane == HEAD_DIM + 2, slope * hi,
                                jnp.where(lane == HEAD_DIM + 3, slope * lo, zero)))

    qn = qk_norm(q_ref, qg_ref) * (1.0 / math.sqrt(HEAD_DIM))
    kn = qk_norm(k_ref, kg_ref)
    qa_ref[0] = jnp.where(first, qn, q_aug).astype(BF16)
    qa_ref[1] = jnp.where(first, pltpu.roll(qn, HEAD_DIM, axis=1), q_aug).astype(BF16)
    ka_ref[0] = jnp.where(first, kn, k_aug).astype(BF16)
    ka_ref[1] = jnp.where(first, pltpu.roll(kn, HEAD_DIM, axis=1), k_aug).astype(BF16)

    row = lax.broadcasted_iota(jnp.int32, (tq, tq), 0)
    col = lax.broadcasted_iota(jnp.int32, (tq, tq), 1)
    causal = col <= row
    sg = sg_ref[...]

    for i in range(s_len // tq):
        kv = (i + 1) * tq

        def probs(c):
            s = lax.dot_general(qa_ref[c, i * tq:(i + 1) * tq, :], ka_ref[c, :kv, :], NT_DIMS,
                                preferred_element_type=F32)
            diag = jnp.where(causal, s[:, kv - tq:], -jnp.inf)
            s = diag if i == 0 else jnp.concatenate([s[:, :kv - tq], diag], axis=1)
            p = jnp.exp(s - jnp.max(s, axis=-1, keepdims=True))
            return p, jnp.sum(p, axis=-1, keepdims=True)

        p1, l1 = probs(0)
        p2, l2 = probs(1)
        a = p1 * (1.0 / l1) - p2 * (lam / l2)
        o = jnp.dot(a.astype(BF16), v_ref[:kv, :], preferred_element_type=F32)
        ms = jnp.mean(o * o, axis=-1, keepdims=True)
        o = o * lax.rsqrt(ms + EPS) * sg * (1.0 - lam_init)
        o_ref[i * tq:(i + 1) * tq, :] = o.astype(o_ref.dtype)


def _attention(proj, qg2, kg2, lq1, lk1, lq2, lk2, sg, *, batch, seq, n_heads, lam_init, tq):
    t = proj.shape[0]
    small = lambda w: pl.BlockSpec((1, w), lambda b, h: (0, 0))
    kern = functools.partial(_attn_kernel, tq=tq, n_heads=n_heads, lam_init=lam_init)
    return pl.pallas_call(
        kern,
        grid=(batch, n_heads),
        in_specs=[pl.BlockSpec((seq, V_DIM), lambda b, h: (b, h)),
                  pl.BlockSpec((seq, V_DIM), lambda b, h: (b, n_heads + h)),
                  pl.BlockSpec((seq, V_DIM), lambda b, h: (b, 2 * n_heads + h)),
                  small(V_DIM), small(V_DIM), small(HEAD_DIM), small(HEAD_DIM), small(HEAD_DIM),
                  small(HEAD_DIM), small(V_DIM)],
        out_specs=pl.BlockSpec((seq, V_DIM), lambda b, h: (b, h)),
        out_shape=jax.ShapeDtypeStruct((t, n_heads * V_DIM), BF16),
        scratch_shapes=[pltpu.VMEM((2, seq, V_DIM), BF16), pltpu.VMEM((2, seq, V_DIM), BF16)],
        compiler_params=_params("parallel", "parallel"),
        name="diff_attn",
    )(proj, proj, proj, qg2, kg2, lq1, lk1, lq2, lk2, sg)


PEER_CHUNKS = 4
DMA_QUEUES = 2
WAIT_GROUP = 8
SMEM_ROWS = 16
HALO = 16


def _mix_out_kernel(x_ref, a_ref, cb_ref, cc_ref, cx_ref, hc_ref, hx_ref, cw_ref, w_ref, o_ref, u_ref,
                    *, tiles_per_seq):
    tm = x_ref.shape[0]
    i = pl.program_id(0)
    u_ref[HALO:, :] = cc_ref[...].astype(F32) * cx_ref[...].astype(F32)
    halo = hc_ref[...].astype(F32) * hx_ref[...].astype(F32)
    u_ref[:HALO, :] = jnp.where(lax.rem(i, tiles_per_seq) == 0, 0.0, halo)
    cw = cw_ref[...]
    y = u_ref[HALO - 2:HALO - 2 + tm, :] * cw[0:1, :]
    y = y + u_ref[HALO - 1:HALO - 1 + tm, :] * cw[1:2, :]
    y = y + u_ref[HALO:, :] * cw[2:3, :]
    c = (cb_ref[...].astype(F32) * y).astype(BF16)
    aw = a_ref.shape[1]
    o = jnp.dot(a_ref[...], w_ref[:aw, :], preferred_element_type=F32)
    o = o + jnp.dot(c, w_ref[aw:, :], preferred_element_type=F32)
    o_ref[...] = x_ref[...] + o


def _mix_out(x2, a, proj, conv_w, w_out_bf, *, seq, tm, attn_w, conv_wd):
    t, d = x2.shape
    cblk = (3 * attn_w) // conv_wd
    hb = tm // HALO
    halo_map = lambda off: (lambda i: (jnp.maximum(i * hb - 1, 0), cblk + off))
    kern = functools.partial(_mix_out_kernel, tiles_per_seq=seq // tm)
    return pl.pallas_call(
        kern,
        grid=(t // tm,),
        in_specs=[pl.BlockSpec((tm, d), lambda i: (i, 0)),
                  pl.BlockSpec((tm, attn_w), lambda i: (i, 0)),
                  pl.BlockSpec((tm, conv_wd), lambda i: (i, cblk)),
                  pl.BlockSpec((tm, conv_wd), lambda i: (i, cblk + 1)),
                  pl.BlockSpec((tm, conv_wd), lambda i: (i, cblk + 2)),
                  pl.BlockSpec((HALO, conv_wd), halo_map(1)),
                  pl.BlockSpec((HALO, conv_wd), halo_map(2)),
                  pl.BlockSpec((CONV_K, conv_wd), lambda i: (0, 0)),
                  pl.BlockSpec((attn_w + conv_wd, d), lambda i: (0, 0))],
        out_specs=pl.BlockSpec((tm, d), lambda i: (i, 0)),
        out_shape=jax.ShapeDtypeStruct((t, d), F32),
        scratch_shapes=[pltpu.VMEM((tm + HALO, conv_wd), F32)],
        compiler_params=_params("parallel"),
        name="mix_out",
    )(x2, a, proj, proj, proj, proj, proj, conv_w, w_out_bf)


def _top_rows(work, k, ridx=None):
    n, tm = work.shape
    if ridx is None:
        ridx = lax.broadcasted_iota(jnp.int32, (n, tm), 0)
    krow = lax.broadcasted_iota(jnp.int32, (k, tm), 0)
    vals = jnp.zeros((k, tm), F32)
    idxs = jnp.zeros((k, tm), jnp.int32)
    for r in range(k):
        m = jnp.max(work, axis=0, keepdims=True)
        am = jnp.min(jnp.where(work == m, ridx, jnp.iinfo(jnp.int32).max), axis=0, keepdims=True)
        vals = jnp.where(krow == r, m, vals)
        idxs = jnp.where(krow == r, am, idxs)
        work = jnp.where(ridx == am, -jnp.inf, work)
    return vals, idxs


def _candidates(s1, s2, k):
    tm = s1.shape[1]
    sub = lax.broadcasted_iota(jnp.int32, (SUBLANES, tm), 0)
    root = math.isqrt(k)
    vals, idxs = [], []

    def add(v, flat, ok):
        vals.append(jnp.where(ok, v, -jnp.inf))
        idxs.append(jnp.where(ok, flat, k * k))

    for i in range(root):
        nj = k // (i + 1)
        for j0 in range(0, nj, SUBLANES):
            j = sub + j0
            add(s1[i:i + 1, :] + s2[j0:j0 + SUBLANES, :], i * k + j, j < nj)
    for j in range(k // (root + 1)):
        ni = k // (j + 1)
        for i0 in range(root // SUBLANES * SUBLANES, ni, SUBLANES):
            i = sub + i0
            add(s1[i0:i0 + SUBLANES, :] + s2[j:j + 1, :], i * k + j, (i >= root) & (i < ni))
    return jnp.concatenate(vals, axis=0), jnp.concatenate(idxs, axis=0)


def _select_rows(table, sel, k):
    out = jnp.zeros(sel.shape, table.dtype)
    for i in range(k):
        out = jnp.where(sel == i, table[i:i + 1, :], out)
    return out


def _route_kernel(x_ref, g_ref, wq_ref, sk_ref, h_ref, e_ref, gate_ref):
    x = x_ref[...]
    ms = jnp.mean(x * x, axis=-1, keepdims=True)
    h = x * lax.rsqrt(ms + EPS) * g_ref[...]
    h_ref[...] = h
    q = jnp.dot(h.astype(BF16), wq_ref[...], preferred_element_type=F32)
    k = PEER_TOPK
    experts, gates = [], []
    for hd in range(PEER_HEADS):
        sv, si = [], []
        for c in range(2):
            g = hd * 2 + c
            qg = q[:, g * PEER_HALF:(g + 1) * PEER_HALF].astype(BF16)
            sc = lax.dot_general(sk_ref[g], qg, NT_DIMS, preferred_element_type=F32)
            v, ix = _top_rows(sc, k)
            sv.append(v)
            si.append(ix)
        cand, flat = _candidates(sv[0], sv[1], k)
        top, idx = _top_rows(cand, k, flat)
        e1 = _select_rows(si[0], lax.shift_right_logical(idx, K_SHIFT), k)
        e2 = _select_rows(si[1], idx & (k - 1), k)
        experts.append(e1 * PEER_KEYS + e2)
        p = jnp.exp(top - top[0:1, :])
        gates.append(p / jnp.sum(p, axis=0, keepdims=True))
    e_ref[...] = jnp.concatenate(experts, axis=0).T
    gate_ref[...] = jnp.concatenate(gates, axis=0).T


def _route(x1, g, wq_bf, sk_bf, *, tm):
    t, d = x1.shape
    nq = wq_bf.shape[1]
    ng, nk, nh = sk_bf.shape
    ek = PEER_HEADS * PEER_TOPK
    return pl.pallas_call(
        _route_kernel,
        grid=(t // tm,),
        in_specs=[pl.BlockSpec((tm, d), lambda i: (i, 0)),
                  pl.BlockSpec((1, d), lambda i: (0, 0)),
                  pl.BlockSpec((d, nq), lambda i: (0, 0)),
                  pl.BlockSpec((ng, nk, nh), lambda i: (0, 0, 0))],
        out_specs=[pl.BlockSpec((tm, d), lambda i: (i, 0)),
                   pl.BlockSpec((tm, ek), lambda i: (i, 0)),
                   pl.BlockSpec((tm, ek), lambda i: (i, 0))],
        out_shape=[jax.ShapeDtypeStruct((t, d), F32),
                   jax.ShapeDtypeStruct((t, ek), jnp.int32),
                   jax.ShapeDtypeStruct((t, ek), F32)],
        compiler_params=_params("parallel"),
        name="peer_route",
    )(x1, g, wq_bf, sk_bf)


def _gelu(x):
    return 0.5 * x * (1.0 + lax.erf(x * (1.0 / math.sqrt(2.0))))


def _peer_kernel(e_ref, en_ref, g2_ref, h_ref, x_ref, tab_ref, o_ref, buf_ref, sem_ref, *, n_buf):
    tb, ek = e_ref.shape
    half = tab_ref.shape[2] // 2
    step = pl.program_id(0)
    last_step = pl.num_programs(0) - 1
    even = lax.broadcasted_iota(jnp.int32, (1, 2 * ek), 1) % 2 == 0

    cw = half // PEER_CHUNKS
    per = ek // PEER_CHUNKS
    look = n_buf - 3

    def issue(idx_ref, t, slot, k0, k1):
        for k in range(k0, k1):
            pltpu.make_async_copy(tab_ref.at[idx_ref[t, k]], buf_ref.at[slot, pl.ds(2 * k, 2)],
                                  sem_ref.at[slot]).start(priority=k % DMA_QUEUES)

    def wait(slot):
        pltpu.make_async_copy(buf_ref.at[slot], buf_ref.at[slot], sem_ref.at[slot]).wait()

    def store(t, o):
        o_ref[pl.ds(t, 1), :half] = x_ref[pl.ds(t, 1), :half] + o[0:1, :]
        o_ref[pl.ds(t, 1), half:] = x_ref[pl.ds(t, 1), half:] + o[1:2, :]

    def s1_lhs(t):
        hrow = h_ref[pl.ds(t, 1), :].astype(BF16)
        return jnp.concatenate([hrow[:, :half], hrow[:, half:]], axis=0)

    def gate_act(r, t):
        s = r[0:1, :] + pltpu.roll(r[1:2, :], 2 * ek - 1, axis=1)
        act = jnp.where(even, _gelu(s) * g2_ref[pl.ds(t, 1), :], 0.0)
        return jnp.concatenate([act, pltpu.roll(act, 1, axis=1)], axis=0).astype(BF16)

    def body(t, slot, carry, *, s1=True, act=True, s2=True, row_out=None, nxt=None):
        r_prev, a2_prev, o_prev = carry
        if s1:
            if slot % WAIT_GROUP == 0:
                for w in range(WAIT_GROUP):
                    wait(slot + w)
            hh = s1_lhs(t)
        a2 = gate_act(r_prev, t - 1) if act else a2_prev
        slot_v = (slot - 2) % n_buf
        r = jnp.zeros((2, 2 * ek), F32)
        outs = []
        for c in range(PEER_CHUNKS):
            if s1:
                rows = buf_ref[slot, :, c * cw:(c + 1) * cw]
                r = r + lax.dot_general(hh[:, c * cw:(c + 1) * cw], rows, NT_DIMS,
                                        preferred_element_type=F32)
            if s2:
                rows = buf_ref[slot_v, :, half + c * cw:half + (c + 1) * cw]
                outs.append(jnp.dot(a2_prev, rows, preferred_element_type=F32))
            if nxt is not None:
                issue(*nxt, c * per, (c + 1) * per)
        if row_out is not None:
            store(row_out, o_prev)
        return r, a2, (jnp.concatenate(outs, axis=1) if s2 else o_prev)

    @pl.when(step == 0)
    def _():
        for j in range(look):
            issue(e_ref, j, j, 0, ek)

    def future(t):
        tn = t + look
        return (e_ref, tn, tn % n_buf) if tn < tb else (en_ref, tn - tb, tn % n_buf)

    carry = (jnp.zeros((2, 2 * ek), F32), jnp.zeros((2, 2 * ek), BF16), jnp.zeros((2, half), F32))
    carry = body(0, 0, carry, act=False, s2=False, nxt=future(0))
    carry = body(1, 1, carry, s2=False, nxt=future(1))

    def group(g, carry):
        for j in range(n_buf):
            t = 2 + g * n_buf + j
            slot = (2 + j) % n_buf
            row_out = t - 3 if j else jnp.maximum(t - 3, 0)
            carry = body(t, slot, carry, row_out=row_out, nxt=(e_ref, t + look, (slot + look) % n_buf))
        return carry

    n_groups = (tb - 2 - look) // n_buf
    carry = lax.fori_loop(0, n_groups, group, carry)
    for t in range(2 + n_groups * n_buf, tb):
        carry = body(t, t % n_buf, carry, row_out=t - 3, nxt=future(t))
    carry = body(tb, tb % n_buf, carry, s1=False, row_out=tb - 3)
    carry = body(tb + 1, (tb + 1) % n_buf, carry, s1=False, act=False, row_out=tb - 2)
    store(tb - 1, carry[2])

    @pl.when(step == last_step)
    def _():
        for j in range(look):
            wait(j)


def _peer(experts, gates2, h, x1, table, *, tb, n_buf):
    t, d = x1.shape
    ek = experts.shape[1]
    n_steps = t // tb
    assert tb % n_buf == 0 and n_buf - 3 <= SMEM_ROWS and tb % SMEM_ROWS == 0 and n_buf % WAIT_GROUP == 0
    assert n_buf & (n_buf - 1) == 0
    nb = tb // SMEM_ROWS
    kern = functools.partial(_peer_kernel, n_buf=n_buf)
    return pl.pallas_call(
        kern,
        grid=(n_steps,),
        in_specs=[pl.BlockSpec((tb, ek), lambda i: (i, 0), memory_space=pltpu.SMEM),
                  pl.BlockSpec((SMEM_ROWS, ek), lambda i: (jnp.minimum(i + 1, n_steps - 1) * nb, 0),
                               memory_space=pltpu.SMEM),
                  pl.BlockSpec((tb, 2 * ek), lambda i: (i, 0)),
                  pl.BlockSpec((tb, d), lambda i: (i, 0)),
                  pl.BlockSpec((tb, d), lambda i: (i, 0)),
                  pl.BlockSpec(memory_space=pl.ANY)],
        out_specs=pl.BlockSpec((tb, d), lambda i: (i, 0)),
        out_shape=jax.ShapeDtypeStruct((t, d), F32),
        scratch_shapes=[pltpu.VMEM((n_buf, 2 * ek, table.shape[2]), BF16),
                        pltpu.SemaphoreType.DMA((n_buf,))],
        compiler_params=_params("arbitrary"),
        name="peer_experts",
    )(experts, experts, gates2, h, x1, table)


def _pack_kernel(u_ref, v_ref, o_ref):
    half = u_ref.shape[1] // 2
    o_ref[:, 0, :half] = u_ref[:, :half].astype(BF16)
    o_ref[:, 0, half:] = v_ref[:, :half].astype(BF16)
    o_ref[:, 1, :half] = u_ref[:, half:].astype(BF16)
    o_ref[:, 1, half:] = v_ref[:, half:].astype(BF16)


def _pack_table(u, v, *, te):
    e, d = u.shape
    spec = pl.BlockSpec((te, d), lambda i: (i, 0))
    return pl.pallas_call(
        _pack_kernel,
        grid=(e // te,),
        in_specs=[spec, spec],
        out_specs=pl.BlockSpec((te, 2, d), lambda i: (i, 0, 0)),
        out_shape=jax.ShapeDtypeStruct((e, 2, d), BF16),
        compiler_params=_params("parallel"),
        name="peer_pack",
    )(u, v)


def _layer(x2, l, attn_norm_g, w_in, q_norm_g, k_norm_g, lambda_q1, lambda_k1, lambda_q2, lambda_k2,
           subln_g, conv_w, w_out, ffn_norm_g, peer_w_q, peer_subkeys, peer_u, peer_v, *, batch, seq):
    t, d = x2.shape
    attn_w = d // 2
    conv_wd = d - attn_w
    n_heads = attn_w // V_DIM
    lam_init = 0.8 - 0.6 * math.exp(-0.3 * l)
    row = lambda p: p[l].reshape(1, -1).astype(F32)
    row2 = lambda p: jnp.tile(p[l].reshape(1, -1).astype(F32), (1, 2))

    proj = _in_proj(x2, row(attn_norm_g), w_in[l].astype(BF16), tm=512, tn=1536)
    a = _attention(proj, row2(q_norm_g), row2(k_norm_g), row(lambda_q1), row(lambda_k1), row(lambda_q2),
                   row(lambda_k2), row(subln_g), batch=batch, seq=seq, n_heads=n_heads,
                   lam_init=lam_init, tq=256)
    x1 = _mix_out(x2, a, proj, conv_w[l].astype(F32), w_out[l].astype(BF16), seq=seq, tm=256,
                  attn_w=attn_w, conv_wd=conv_wd)
    sk = peer_subkeys[l].reshape(PEER_HEADS * 2, PEER_KEYS, PEER_HALF).astype(BF16)
    h, experts, gates = _route(x1, row(ffn_norm_g), peer_w_q[l].astype(BF16), sk, tm=256)
    gates2 = jnp.stack([gates, jnp.zeros_like(gates)], axis=-1).reshape(t, -1)
    table = _pack_table(peer_u[l], peer_v[l], te=512)
    return _peer(experts, gates2, h, x1, table, tb=128, n_buf=16)


def kernel(x, attn_norm_g, w_in, q_norm_g, k_norm_g, lambda_q1, lambda_k1, lambda_q2, lambda_k2, subln_g,
           conv_w, w_out, ffn_norm_g, peer_w_q, peer_subkeys, peer_u, peer_v):
    b, s, d = x.shape
    x2 = x.reshape(b * s, d)
    for l in range(w_in.shape[0]):
        x2 = _layer(x2, l, attn_norm_g, w_in, q_norm_g, k_norm_g, lambda_q1, lambda_k1, lambda_q2,
                    lambda_k2, subln_g, conv_w, w_out, ffn_norm_g, peer_w_q, peer_subkeys, peer_u, peer_v,
                    batch=b, seq=s)
    return x2.reshape(b, s, d)
```

```python
import functools
import math

import jax
import jax.numpy as jnp
from jax import lax
from jax.experimental import pallas as pl
from jax.experimental.pallas import tpu as pltpu

EPS = 1e-6
HEAD_DIM = 64
V_DIM = 2 * HEAD_DIM
CONV_K = 3
PEER_HEADS = 8
PEER_KEYS = 128
PEER_HALF = 128
PEER_TOPK = 16
K_SHIFT = 4
LANES = 128
SUBLANES = 8
ALIBI_SPLIT = 256
VMEM_LIMIT = 56 * 1024 * 1024

F32 = jnp.float32
BF16 = jnp.bfloat16
NT_DIMS = (((1,), (1,)), ((), ()))


def _params(*sem):
    return pltpu.CompilerParams(dimension_semantics=sem, vmem_limit_bytes=VMEM_LIMIT)


def _in_proj_kernel(x_ref, g_ref, w_ref, o_ref, h_ref):
    @pl.when(pl.program_id(1) == 0)
    def _():
        x = x_ref[...]
        ms = jnp.mean(x * x, axis=-1, keepdims=True)
        h_ref[...] = (x * lax.rsqrt(ms + EPS) * g_ref[...]).astype(BF16)

    o_ref[...] = jnp.dot(h_ref[...], w_ref[...], preferred_element_type=F32).astype(o_ref.dtype)


def _in_proj(x2, g, w_bf, *, tm, tn):
    t, d = x2.shape
    n = w_bf.shape[1]
    return pl.pallas_call(
        _in_proj_kernel,
        grid=(t // tm, n // tn),
        in_specs=[pl.BlockSpec((tm, d), lambda i, j: (i, 0)),
                  pl.BlockSpec((1, d), lambda i, j: (0, 0)),
                  pl.BlockSpec((d, tn), lambda i, j: (0, j))],
        out_specs=pl.BlockSpec((tm, tn), lambda i, j: (i, j)),
        out_shape=jax.ShapeDtypeStruct((t, n), BF16),
        scratch_shapes=[pltpu.VMEM((tm, d), BF16)],
        compiler_params=_params("parallel", "arbitrary"),
        name="in_proj",
    )(x2, g, w_bf)


def _attn_kernel(q_ref, k_ref, v_ref, qg_ref, kg_ref, lq1_ref, lk1_ref, lq2_ref, lk2_ref, sg_ref,
                 o_ref, qa_ref, ka_ref, *, tq, n_heads, lam_init):
    s_len = q_ref.shape[0]
    lane = lax.broadcasted_iota(jnp.int32, (1, LANES), 1)
    first = lane < HEAD_DIM

    lam = (jnp.exp(jnp.sum(lq1_ref[...] * lk1_ref[...], axis=-1, keepdims=True))
           - jnp.exp(jnp.sum(lq2_ref[...] * lk2_ref[...], axis=-1, keepdims=True)) + lam_init)
    head = (pl.program_id(1) + 1).astype(F32)
    slope = jnp.exp2(jnp.full((1, 1), -8.0 / n_heads, F32) * head)

    def qk_norm(x_ref, g_ref):
        x = x_ref[...].astype(F32)
        x2 = x * x
        s1 = jnp.sum(jnp.where(first, x2, 0.0), axis=-1, keepdims=True)
        s2 = jnp.sum(jnp.where(first, 0.0, x2), axis=-1, keepdims=True)
        r = jnp.where(first, lax.rsqrt(s1 / HEAD_DIM + EPS), lax.rsqrt(s2 / HEAD_DIM + EPS))
        return x * r * g_ref[...]

    pos = lax.broadcasted_iota(jnp.int32, (s_len, 1), 0)
    hi = (pos & ~(ALIBI_SPLIT - 1)).astype(F32)
    lo = (pos & (ALIBI_SPLIT - 1)).astype(F32)
    one = jnp.ones((s_len, 1), F32)
    zero = jnp.zeros((s_len, 1), F32)
    q_aug = jnp.where(lane == HEAD_DIM, -slope * hi,
                      jnp.where(lane == HEAD_DIM + 1, -slope * lo,
                                jnp.where(lane <= HEAD_DIM + 3, one, zero)))
    k_aug = jnp.where(lane <= HEAD_DIM + 1, one,
                      jnp.where(lane == HEAD_DIM + 2, slope * hi,
                                jnp.where(lane == HEAD_DIM + 3, slope * lo, zero)))

    qn = qk_norm(q_ref, qg_ref) * (1.0 / math.sqrt(HEAD_DIM))
    kn = qk_norm(k_ref, kg_ref)
    qa_ref[0] = jnp.where(first, qn, q_aug).astype(BF16)
    qa_ref[1] = jnp.where(first, pltpu.roll(qn, HEAD_DIM, axis=1), q_aug).astype(BF16)
    ka_ref[0] = jnp.where(first, kn, k_aug).astype(BF16)
    ka_ref[1] = jnp.where(first, pltpu.roll(kn, HEAD_DIM, axis=1), k_aug).astype(BF16)

    row = lax.broadcasted_iota(jnp.int32, (tq, tq), 0)
    col = lax.broadcasted_iota(jnp.int32, (tq, tq), 1)
    causal = col <= row
    sg = sg_ref[...]

    for i in range(s_len // tq):
        kv = (i + 1) * tq

        def probs(c):
            s = lax.dot_general(qa_ref[c, i * tq:(i + 1) * tq, :], ka_ref[c, :kv, :], NT_DIMS,
                                preferred_element_type=F32)
            diag = jnp.where(causal, s[:, kv - tq:], -jnp.inf)
            s = diag if i == 0 else jnp.concatenate([s[:, :kv - tq], diag], axis=1)
            p = jnp.exp(s - jnp.max(s, axis=-1, keepdims=True))
            return p, jnp.sum(p, axis=-1, keepdims=True)

        p1, l1 = probs(0)
        p2, l2 = probs(1)
        a = p1 * (1.0 / l1) - p2 * (lam / l2)
        o = jnp.dot(a.astype(BF16), v_ref[:kv, :], preferred_element_type=F32)
        ms = jnp.mean(o * o, axis=-1, keepdims=True)
        o = o * lax.rsqrt(ms + EPS) * sg * (1.0 - lam_init)
        o_ref[i * tq:(i + 1) * tq, :] = o.astype(o_ref.dtype)


def _attention(proj, qg2, kg2, lq1, lk1, lq2, lk2, sg, *, batch, seq, n_heads, lam_init, tq):
    t = proj.shape[0]
    small = lambda w: pl.BlockSpec((1, w), lambda b, h: (0, 0))
    kern = functools.partial(_attn_kernel, tq=tq, n_heads=n_heads, lam_init=lam_init)
    return pl.pallas_call(
        kern,
        grid=(batch, n_heads),
        in_specs=[pl.BlockSpec((seq, V_DIM), lambda b, h: (b, h)),
                  pl.BlockSpec((seq, V_DIM), lambda b, h: (b, n_heads + h)),
                  pl.BlockSpec((seq, V_DIM), lambda b, h: (b, 2 * n_heads + h)),
                  small(V_DIM), small(V_DIM), small(HEAD_DIM), small(HEAD_DIM), small(HEAD_DIM),
                  small(HEAD_DIM), small(V_DIM)],
        out_specs=pl.BlockSpec((seq, V_DIM), lambda b, h: (b, h)),
        out_shape=jax.ShapeDtypeStruct((t, n_heads * V_DIM), BF16),
        scratch_shapes=[pltpu.VMEM((2, seq, V_DIM), BF16), pltpu.VMEM((2, seq, V_DIM), BF16)],
        compiler_params=_params("parallel", "parallel"),
        name="diff_attn",
    )(proj, proj, proj, qg2, kg2, lq1, lk1, lq2, lk2, sg)


PEER_CHUNKS = 4
DMA_QUEUES = 2
WAIT_GROUP = 8
SMEM_ROWS = 16
HALO = 16


def _mix_out_kernel(x_ref, a_ref, cb_ref, cc_ref, cx_ref, hc_ref, hx_ref, cw_ref, w_ref, o_ref, u_ref,
                    *, tiles_per_seq):
    tm = x_ref.shape[0]
    i = pl.program_id(0)
    u_ref[HALO:, :] = cc_ref[...].astype(F32) * cx_ref[...].astype(F32)
    halo = hc_ref[...].astype(F32) * hx_ref[...].astype(F32)
    u_ref[:HALO, :] = jnp.where(lax.rem(i, tiles_per_seq) == 0, 0.0, halo)
    cw = cw_ref[...]
    y = u_ref[HALO - 2:HALO - 2 + tm, :] * cw[0:1, :]
    y = y + u_ref[HALO - 1:HALO - 1 + tm, :] * cw[1:2, :]
    y = y + u_ref[HALO:, :] * cw[2:3, :]
    c = (cb_ref[...].astype(F32) * y).astype(BF16)
    aw = a_ref.shape[1]
    o = jnp.dot(a_ref[...], w_ref[:aw, :], preferred_element_type=F32)
    o = o + jnp.dot(c, w_ref[aw:, :], preferred_element_type=F32)
    o_ref[...] = x_ref[...] + o


def _mix_out(x2, a, proj, conv_w, w_out_bf, *, seq, tm, attn_w, conv_wd):
    t, d = x2.shape
    cblk = (3 * attn_w) // conv_wd
    hb = tm // HALO
    halo_map = lambda off: (lambda i: (jnp.maximum(i * hb - 1, 0), cblk + off))
    kern = functools.partial(_mix_out_kernel, tiles_per_seq=seq // tm)
    return pl.pallas_call(
        kern,
        grid=(t // tm,),
        in_specs=[pl.BlockSpec((tm, d), lambda i: (i, 0)),
                  pl.BlockSpec((tm, attn_w), lambda i: (i, 0)),
                  pl.BlockSpec((tm, conv_wd), lambda i: (i, cblk)),
                  pl.BlockSpec((tm, conv_wd), lambda i: (i, cblk + 1)),
                  pl.BlockSpec((tm, conv_wd), lambda i: (i, cblk + 2)),
                  pl.BlockSpec((HALO, conv_wd), halo_map(1)),
                  pl.BlockSpec((HALO, conv_wd), halo_map(2)),
                  pl.BlockSpec((CONV_K, conv_wd), lambda i: (0, 0)),
                  pl.BlockSpec((attn_w + conv_wd, d), lambda i: (0, 0))],
        out_specs=pl.BlockSpec((tm, d), lambda i: (i, 0)),
        out_shape=jax.ShapeDtypeStruct((t, d), F32),
        scratch_shapes=[pltpu.VMEM((tm + HALO, conv_wd), F32)],
        compiler_params=_params("parallel"),
        name="mix_out",
    )(x2, a, proj, proj, proj, proj, proj, conv_w, w_out_bf)


def _top_rows(work, k, ridx=None):
    n, tm = work.shape
    if ridx is None:
        ridx = lax.broadcasted_iota(jnp.int32, (n, tm), 0)
    krow = lax.broadcasted_iota(jnp.int32, (k, tm), 0)
    vals = jnp.zeros((k, tm), F32)
    idxs = jnp.zeros((k, tm), jnp.int32)
    for r in range(k):
        m = jnp.max(work, axis=0, keepdims=True)
        am = jnp.min(jnp.where(work == m, ridx, jnp.iinfo(jnp.int32).max), axis=0, keepdims=True)
        vals = jnp.where(krow == r, m, vals)
        idxs = jnp.where(krow == r, am, idxs)
        work = jnp.where(ridx == am, -jnp.inf, work)
    return vals, idxs


def _candidates(s1, s2, k):
    tm = s1.shape[1]
    sub = lax.broadcasted_iota(jnp.int32, (SUBLANES, tm), 0)
    root = math.isqrt(k)
    vals, idxs = [], []

    def add(v, flat, ok):
        vals.append(jnp.where(ok, v, -jnp.inf))
        idxs.append(jnp.where(ok, flat, k * k))

    for i in range(root):
        nj = k // (i + 1)
        for j0 in range(0, nj, SUBLANES):
            j = sub + j0
            add(s1[i:i + 1, :] + s2[j0:j0 + SUBLANES, :], i * k + j, j < nj)
    for j in range(k // (root + 1)):
        ni = k // (j + 1)
        for i0 in range(root // SUBLANES * SUBLANES, ni, SUBLANES):
            i = sub + i0
            add(s1[i0:i0 + SUBLANES, :] + s2[j:j + 1, :], i * k + j, (i >= root) & (i < ni))
    return jnp.concatenate(vals, axis=0), jnp.concatenate(idxs, axis=0)


def _select_rows(table, sel, k):
    out = jnp.zeros(sel.shape, table.dtype)
    for i in range(k):
        out = jnp.where(sel == i, table[i:i + 1, :], out)
    return out


def _route_kernel(x_ref, g_ref, wq_ref, sk_ref, h_ref, e_ref, gate_ref):
    x = x_ref[...]
    ms = jnp.mean(x * x, axis=-1, keepdims=True)
    h = x * lax.rsqrt(ms + EPS) * g_ref[...]
    h_ref[...] = h
    q = jnp.dot(h.astype(BF16), wq_ref[...], preferred_element_type=F32)
    k = PEER_TOPK
    experts, gates = [], []
    for hd in range(PEER_HEADS):
        sv, si = [], []
        for c in range(2):
            g = hd * 2 + c
            qg = q[:, g * PEER_HALF:(g + 1) * PEER_HALF].astype(BF16)
            sc = lax.dot_general(sk_ref[g], qg, NT_DIMS, preferred_element_type=F32)
            v, ix = _top_rows(sc, k)
            sv.append(v)
            si.append(ix)
        cand, flat = _candidates(sv[0], sv[1], k)
        top, idx = _top_rows(cand, k, flat)
        e1 = _select_rows(si[0], lax.shift_right_logical(idx, K_SHIFT), k)
        e2 = _select_rows(si[1], idx & (k - 1), k)
        experts.append(e1 * PEER_KEYS + e2)
        p = jnp.exp(top - top[0:1, :])
        gates.append(p / jnp.sum(p, axis=0, keepdims=True))
    e_ref[...] = jnp.concatenate(experts, axis=0).T
    gate_ref[...] = jnp.concatenate(gates, axis=0).T


def _route(x1, g, wq_bf, sk_bf, *, tm):
    t, d = x1.shape
    nq = wq_bf.shape[1]
    ng, nk, nh = sk_bf.shape
    ek = PEER_HEADS * PEER_TOPK
    return pl.pallas_call(
        _route_kernel,
        grid=(t // tm,),
        in_specs=[pl.BlockSpec((tm, d), lambda i: (i, 0)),
                  pl.BlockSpec((1, d), lambda i: (0, 0)),
                  pl.BlockSpec((d, nq), lambda i: (0, 0)),
                  pl.BlockSpec((ng, nk, nh), lambda i: (0, 0, 0))],
        out_specs=[pl.BlockSpec((tm, d), lambda i: (i, 0)),
                   pl.BlockSpec((tm, ek), lambda i: (i, 0)),
                   pl.BlockSpec((tm, ek), lambda i: (i, 0))],
        out_shape=[jax.ShapeDtypeStruct((t, d), F32),
                   jax.ShapeDtypeStruct((t, ek), jnp.int32),
                   jax.ShapeDtypeStruct((t, ek), F32)],
        compiler_params=_params("parallel"),
        name="peer_route",
    )(x1, g, wq_bf, sk_bf)


def _gelu(x):
    return 0.5 * x * (1.0 + lax.erf(x * (1.0 / math.sqrt(2.0))))


def _peer_kernel(e_ref, g2_ref, h_ref, x_ref, tab_ref, o_ref, buf_ref, sem_ref, *, n_buf):
    tb = o_ref.shape[0]
    ek = e_ref.shape[2]
    half = tab_ref.shape[2] // 2
    step = pl.program_id(0)
    last_step = pl.num_programs(0) - 1
    even = lax.broadcasted_iota(jnp.int32, (1, 2 * ek), 1) % 2 == 0

    cw = half // PEER_CHUNKS
    per = ek // PEER_CHUNKS
    look = n_buf - 3

    def row_copy(t, k, slot):
        return pltpu.make_async_copy(tab_ref.at[e_ref[0, t, k]], buf_ref.at[slot, pl.ds(2 * k, 2)],
                                     sem_ref.at[slot])

    def wait(slot):
        pltpu.make_async_copy(buf_ref.at[slot], buf_ref.at[slot], sem_ref.at[slot]).wait()

    def store(t, o):
        o_ref[pl.ds(t, 1), :half] = x_ref[pl.ds(t, 1), :half] + o[0:1, :]
        o_ref[pl.ds(t, 1), half:] = x_ref[pl.ds(t, 1), half:] + o[1:2, :]

    def s1_lhs(t):
        hrow = h_ref[pl.ds(t, 1), :].astype(BF16)
        return jnp.concatenate([hrow[:, :half], hrow[:, half:]], axis=0)

    def gate_act(r, t):
        s = r[0:1, :] + pltpu.roll(r[1:2, :], 2 * ek - 1, axis=1)
        act = jnp.where(even, _gelu(s) * g2_ref[pl.ds(t, 1), :], 0.0)
        return jnp.concatenate([act, pltpu.roll(act, 1, axis=1)], axis=0).astype(BF16)

    def body(t, slot, carry, *, s1=True, act=True, t_act=None, row_out=None):
        r_prev, a2_prev, o_prev = carry
        if s1:
            if slot % WAIT_GROUP == 0:
                for w in range(WAIT_GROUP):
                    wait(slot + w)
            hh = s1_lhs(t)
        a2 = gate_act(r_prev, t_act) if act else a2_prev
        slot_v, slot_next = (slot - 2) % n_buf, (slot + look) % n_buf
        r = jnp.zeros((2, 2 * ek), F32)
        outs = []
        for c in range(PEER_CHUNKS):
            if s1:
                rows = buf_ref[slot, :, c * cw:(c + 1) * cw]
                r = r + lax.dot_general(hh[:, c * cw:(c + 1) * cw], rows, NT_DIMS,
                                        preferred_element_type=F32)
            rows = buf_ref[slot_v, :, half + c * cw:half + (c + 1) * cw]
            outs.append(jnp.dot(a2_prev, rows, preferred_element_type=F32))
            if s1:
                for k in range(c * per, (c + 1) * per):
                    row_copy(t + look, k, slot_next).start(priority=k % DMA_QUEUES)
        store(row_out, o_prev)
        return r, a2, jnp.concatenate(outs, axis=1)

    @pl.when(step == 0)
    def _():
        buf_ref[n_buf - 2:, :, :] = jnp.zeros((2,) + buf_ref.shape[1:], BF16)
        for j in range(look):
            for k in range(ek):
                row_copy(j, k, j).start(priority=k % DMA_QUEUES)

    def group(g, carry):
        for j in range(n_buf):
            t = g * n_buf + j
            carry = body(t, j, carry, t_act=t - 1 if j >= 1 else jnp.maximum(t - 1, 0),
                         row_out=t - 3 if j >= 3 else jnp.maximum(t - 3, 0))
        return carry

    carry = (jnp.zeros((2, 2 * ek), F32), jnp.zeros((2, 2 * ek), BF16), jnp.zeros((2, half), F32))
    carry = lax.fori_loop(0, tb // n_buf, group, carry)
    carry = body(tb, 0, carry, s1=False, t_act=tb - 1, row_out=tb - 3)
    carry = body(tb + 1, 1, carry, s1=False, act=False, row_out=tb - 2)
    store(tb - 1, carry[2])

    @pl.when(step == last_step)
    def _():
        for j in range(look):
            wait(j)


def _peer(experts, gates2, h, x1, table, *, tb, n_buf):
    t, d = x1.shape
    ek = experts.shape[1]
    n_steps = t // tb
    assert tb % n_buf == 0 and n_buf % WAIT_GROUP == 0 and n_buf - 3 <= SMEM_ROWS
    e3 = experts.reshape(n_steps, tb, ek)
    e_ext = jnp.concatenate([e3, jnp.roll(e3[:, :SMEM_ROWS], -1, axis=0)], axis=1)
    kern = functools.partial(_peer_kernel, n_buf=n_buf)
    return pl.pallas_call(
        kern,
        grid=(n_steps,),
        in_specs=[pl.BlockSpec((1, tb + SMEM_ROWS, ek), lambda i: (i, 0, 0), memory_space=pltpu.SMEM),
                  pl.BlockSpec((tb, 2 * ek), lambda i: (i, 0)),
                  pl.BlockSpec((tb, d), lambda i: (i, 0)),
                  pl.BlockSpec((tb, d), lambda i: (i, 0)),
                  pl.BlockSpec(memory_space=pl.ANY)],
        out_specs=pl.BlockSpec((tb, d), lambda i: (i, 0)),
        out_shape=jax.ShapeDtypeStruct((t, d), F32),
        scratch_shapes=[pltpu.VMEM((n_buf, 2 * ek, table.shape[2]), BF16),
                        pltpu.SemaphoreType.DMA((n_buf,))],
        compiler_params=_params("arbitrary"),
        name="peer_experts",
    )(e_ext, gates2, h, x1, table)


def _pack_kernel(u_ref, v_ref, o_ref):
    half = u_ref.shape[1] // 2
    o_ref[:, 0, :half] = u_ref[:, :half].astype(BF16)
    o_ref[:, 0, half:] = v_ref[:, :half].astype(BF16)
    o_ref[:, 1, :half] = u_ref[:, half:].astype(BF16)
    o_ref[:, 1, half:] = v_ref[:, half:].astype(BF16)


def _pack_table(u, v, *, te):
    e, d = u.shape
    spec = pl.BlockSpec((te, d), lambda i: (i, 0))
    return pl.pallas_call(
        _pack_kernel,
        grid=(e // te,),
        in_specs=[spec, spec],
        out_specs=pl.BlockSpec((te, 2, d), lambda i: (i, 0, 0)),
        out_shape=jax.ShapeDtypeStruct((e, 2, d), BF16),
        compiler_params=_params("parallel"),
        name="peer_pack",
    )(u, v)


def _layer(x2, l, attn_norm_g, w_in, q_norm_g, k_norm_g, lambda_q1, lambda_k1, lambda_q2, lambda_k2,
           subln_g, conv_w, w_out, ffn_norm_g, peer_w_q, peer_subkeys, peer_u, peer_v, *, batch, seq):
    t, d = x2.shape
    attn_w = d // 2
    conv_wd = d - attn_w
    n_heads = attn_w // V_DIM
    lam_init = 0.8 - 0.6 * math.exp(-0.3 * l)
    row = lambda p: p[l].reshape(1, -1).astype(F32)
    row2 = lambda p: jnp.tile(p[l].reshape(1, -1).astype(F32), (1, 2))

    proj = _in_proj(x2, row(attn_norm_g), w_in[l].astype(BF16), tm=512, tn=1536)
    a = _attention(proj, row2(q_norm_g), row2(k_norm_g), row(lambda_q1), row(lambda_k1), row(lambda_q2),
                   row(lambda_k2), row(subln_g), batch=batch, seq=seq, n_heads=n_heads,
                   lam_init=lam_init, tq=256)
    x1 = _mix_out(x2, a, proj, conv_w[l].astype(F32), w_out[l].astype(BF16), seq=seq, tm=256,
                  attn_w=attn_w, conv_wd=conv_wd)
    sk = peer_subkeys[l].reshape(PEER_HEADS * 2, PEER_KEYS, PEER_HALF).astype(BF16)
    h, experts, gates = _route(x1, row(ffn_norm_g), peer_w_q[l].astype(BF16), sk, tm=256)
    gates2 = jnp.stack([gates, jnp.zeros_like(gates)], axis=-1).reshape(t, -1)
    table = _pack_table(peer_u[l], peer_v[l], te=512)
    return _peer(experts, gates2, h, x1, table, tb=128, n_buf=16)


def kernel(x, attn_norm_g, w_in, q_norm_g, k_norm_g, lambda_q1, lambda_k1, lambda_q2, lambda_k2, subln_g,
           conv_w, w_out, ffn_norm_g, peer_w_q, peer_subkeys, peer_u, peer_v):
    b, s, d = x.shape
    x2 = x.reshape(b * s, d)
    for l in range(w_in.shape[0]):
        x2 = _layer(x2, l, attn_norm_g, w_in, q_norm_g, k_norm_g, lambda_q1, lambda_k1, lambda_q2,
                    lambda_k2, subln_g, conv_w, w_out, ffn_norm_g, peer_w_q, peer_subkeys, peer_u, peer_v,
                    batch=b, seq=s)
    return x2.reshape(b, s, d)
```

```python
import functools
import math

import jax
import jax.numpy as jnp
from jax import lax
from jax.experimental import pallas as pl
from jax.experimental.pallas import tpu as pltpu

EPS = 1e-6
HEAD_DIM = 64
V_DIM = 2 * HEAD_DIM
CONV_K = 3
PEER_HEADS = 8
PEER_KEYS = 128
PEER_HALF = 128
PEER_TOPK = 16
K_SHIFT = 4
LANES = 128
SUBLANES = 8
ALIBI_SPLIT = 256
VMEM_LIMIT = 56 * 1024 * 1024

F32 = jnp.float32
BF16 = jnp.bfloat16
NT_DIMS = (((1,), (1,)), ((), ()))


def _params(*sem):
    return pltpu.CompilerParams(dimension_semantics=sem, vmem_limit_bytes=VMEM_LIMIT)


def _in_proj_kernel(x_ref, g_ref, w_ref, o_ref, h_ref):
    @pl.when(pl.program_id(1) == 0)
    def _():
        x = x_ref[...]
        ms = jnp.mean(x * x, axis=-1, keepdims=True)
        h_ref[...] = (x * lax.rsqrt(ms + EPS) * g_ref[...]).astype(BF16)

    o_ref[...] = jnp.dot(h_ref[...], w_ref[...], preferred_element_type=F32).astype(o_ref.dtype)


def _in_proj(x2, g, w_bf, *, tm, tn):
    t, d = x2.shape
    n = w_bf.shape[1]
    return pl.pallas_call(
        _in_proj_kernel,
        grid=(t // tm, n // tn),
        in_specs=[pl.BlockSpec((tm, d), lambda i, j: (i, 0)),
                  pl.BlockSpec((1, d), lambda i, j: (0, 0)),
                  pl.BlockSpec((d, tn), lambda i, j: (0, j))],
        out_specs=pl.BlockSpec((tm, tn), lambda i, j: (i, j)),
        out_shape=jax.ShapeDtypeStruct((t, n), BF16),
        scratch_shapes=[pltpu.VMEM((tm, d), BF16)],
        compiler_params=_params("parallel", "arbitrary"),
        name="in_proj",
    )(x2, g, w_bf)


def _attn_kernel(q_ref, k_ref, v_ref, qg_ref, kg_ref, lq1_ref, lk1_ref, lq2_ref, lk2_ref, sg_ref,
                 o_ref, qa_ref, ka_ref, *, tq, n_heads, lam_init):
    s_len = q_ref.shape[0]
    lane = lax.broadcasted_iota(jnp.int32, (1, LANES), 1)
    first = lane < HEAD_DIM

    lam = (jnp.exp(jnp.sum(lq1_ref[...] * lk1_ref[...], axis=-1, keepdims=True))
           - jnp.exp(jnp.sum(lq2_ref[...] * lk2_ref[...], axis=-1, keepdims=True)) + lam_init)
    head = (pl.program_id(1) + 1).astype(F32)
    slope = jnp.exp2(jnp.full((1, 1), -8.0 / n_heads, F32) * head)

    def qk_norm(x_ref, g_ref):
        x = x_ref[...].astype(F32)
        x2 = x * x
        s1 = jnp.sum(jnp.where(first, x2, 0.0), axis=-1, keepdims=True)
        s2 = jnp.sum(jnp.where(first, 0.0, x2), axis=-1, keepdims=True)
        r = jnp.where(first, lax.rsqrt(s1 / HEAD_DIM + EPS), lax.rsqrt(s2 / HEAD_DIM + EPS))
        return x * r * g_ref[...]

    pos = lax.broadcasted_iota(jnp.int32, (s_len, 1), 0)
    hi = (pos & ~(ALIBI_SPLIT - 1)).astype(F32)
    lo = (pos & (ALIBI_SPLIT - 1)).astype(F32)
    one = jnp.ones((s_len, 1), F32)
    zero = jnp.zeros((s_len, 1), F32)
    q_aug = jnp.where(lane == HEAD_DIM, -slope * hi,
                      jnp.where(lane == HEAD_DIM + 1, -slope * lo,
                                jnp.where(lane <= HEAD_DIM + 3, one, zero)))
    k_aug = jnp.where(lane <= HEAD_DIM + 1, one,
                      jnp.where(lane == HEAD_DIM + 2, slope * hi,
                                jnp.where(lane == HEAD_DIM + 3, slope * lo, zero)))

    qn = qk_norm(q_ref, qg_ref) * (1.0 / math.sqrt(HEAD_DIM))
    kn = qk_norm(k_ref, kg_ref)
    qa_ref[0] = jnp.where(first, qn, q_aug).astype(BF16)
    qa_ref[1] = jnp.where(first, pltpu.roll(qn, HEAD_DIM, axis=1), q_aug).astype(BF16)
    ka_ref[0] = jnp.where(first, kn, k_aug).astype(BF16)
    ka_ref[1] = jnp.where(first, pltpu.roll(kn, HEAD_DIM, axis=1), k_aug).astype(BF16)

    row = lax.broadcasted_iota(jnp.int32, (tq, tq), 0)
    col = lax.broadcasted_iota(jnp.int32, (tq, tq), 1)
    causal = col <= row
    sg = sg_ref[...]

    for i in range(s_len // tq):
        kv = (i + 1) * tq

        def probs(c):
            s = lax.dot_general(qa_ref[c, i * tq:(i + 1) * tq, :], ka_ref[c, :kv, :], NT_DIMS,
                                preferred_element_type=F32)
            diag = jnp.where(causal, s[:, kv - tq:], -jnp.inf)
            s = diag if i == 0 else jnp.concatenate([s[:, :kv - tq], diag], axis=1)
            p = jnp.exp(s - jnp.max(s, axis=-1, keepdims=True))
            return p, jnp.sum(p, axis=-1, keepdims=True)

        p1, l1 = probs(0)
        p2, l2 = probs(1)
        o1 = jnp.dot(p1.astype(BF16), v_ref[:kv, :], preferred_element_type=F32)
        o2 = jnp.dot(p2.astype(BF16), v_ref[:kv, :], preferred_element_type=F32)
        o = o1 * (1.0 / l1) - o2 * (lam / l2)
        ms = jnp.mean(o * o, axis=-1, keepdims=True)
        o = o * lax.rsqrt(ms + EPS) * sg * (1.0 - lam_init)
        o_ref[i * tq:(i + 1) * tq, :] = o.astype(o_ref.dtype)


def _attention(proj, qg2, kg2, lq1, lk1, lq2, lk2, sg, *, batch, seq, n_heads, lam_init, tq):
    t = proj.shape[0]
    small = lambda w: pl.BlockSpec((1, w), lambda b, h: (0, 0))
    kern = functools.partial(_attn_kernel, tq=tq, n_heads=n_heads, lam_init=lam_init)
    return pl.pallas_call(
        kern,
        grid=(batch, n_heads),
        in_specs=[pl.BlockSpec((seq, V_DIM), lambda b, h: (b, h)),
                  pl.BlockSpec((seq, V_DIM), lambda b, h: (b, n_heads + h)),
                  pl.BlockSpec((seq, V_DIM), lambda b, h: (b, 2 * n_heads + h)),
                  small(V_DIM), small(V_DIM), small(HEAD_DIM), small(HEAD_DIM), small(HEAD_DIM),
                  small(HEAD_DIM), small(V_DIM)],
        out_specs=pl.BlockSpec((seq, V_DIM), lambda b, h: (b, h)),
        out_shape=jax.ShapeDtypeStruct((t, n_heads * V_DIM), BF16),
        scratch_shapes=[pltpu.VMEM((2, seq, V_DIM), BF16), pltpu.VMEM((2, seq, V_DIM), BF16)],
        compiler_params=_params("parallel", "parallel"),
        name="diff_attn",
    )(proj, proj, proj, qg2, kg2, lq1, lk1, lq2, lk2, sg)


PEER_CHUNKS = 4
DMA_QUEUES = 2
WAIT_GROUP = 8
SMEM_ROWS = 16
HALO = 16


def _mix_out_kernel(x_ref, a_ref, cb_ref, cc_ref, cx_ref, hc_ref, hx_ref, cw_ref, w_ref, o_ref, u_ref,
                    *, tiles_per_seq):
    tm = x_ref.shape[0]
    i = pl.program_id(0)
    u_ref[HALO:, :] = cc_ref[...].astype(F32) * cx_ref[...].astype(F32)
    halo = hc_ref[...].astype(F32) * hx_ref[...].astype(F32)
    u_ref[:HALO, :] = jnp.where(lax.rem(i, tiles_per_seq) == 0, 0.0, halo)
    cw = cw_ref[...]
    y = u_ref[HALO - 2:HALO - 2 + tm, :] * cw[0:1, :]
    y = y + u_ref[HALO - 1:HALO - 1 + tm, :] * cw[1:2, :]
    y = y + u_ref[HALO:, :] * cw[2:3, :]
    c = (cb_ref[...].astype(F32) * y).astype(BF16)
    aw = a_ref.shape[1]
    o = jnp.dot(a_ref[...], w_ref[:aw, :], preferred_element_type=F32)
    o = o + jnp.dot(c, w_ref[aw:, :], preferred_element_type=F32)
    o_ref[...] = x_ref[...] + o


def _mix_out(x2, a, proj, conv_w, w_out_bf, *, seq, tm, attn_w, conv_wd):
    t, d = x2.shape
    cblk = (3 * attn_w) // conv_wd
    hb = tm // HALO
    halo_map = lambda off: (lambda i: (jnp.maximum(i * hb - 1, 0), cblk + off))
    kern = functools.partial(_mix_out_kernel, tiles_per_seq=seq // tm)
    return pl.pallas_call(
        kern,
        grid=(t // tm,),
        in_specs=[pl.BlockSpec((tm, d), lambda i: (i, 0)),
                  pl.BlockSpec((tm, attn_w), lambda i: (i, 0)),
                  pl.BlockSpec((tm, conv_wd), lambda i: (i, cblk)),
                  pl.BlockSpec((tm, conv_wd), lambda i: (i, cblk + 1)),
                  pl.BlockSpec((tm, conv_wd), lambda i: (i, cblk + 2)),
                  pl.BlockSpec((HALO, conv_wd), halo_map(1)),
                  pl.BlockSpec((HALO, conv_wd), halo_map(2)),
                  pl.BlockSpec((CONV_K, conv_wd), lambda i: (0, 0)),
                  pl.BlockSpec((attn_w + conv_wd, d), lambda i: (0, 0))],
        out_specs=pl.BlockSpec((tm, d), lambda i: (i, 0)),
        out_shape=jax.ShapeDtypeStruct((t, d), F32),
        scratch_shapes=[pltpu.VMEM((tm + HALO, conv_wd), F32)],
        compiler_params=_params("parallel"),
        name="mix_out",
    )(x2, a, proj, proj, proj, proj, proj, conv_w, w_out_bf)


def _top_rows(work, ridx, k):
    tm = work[0].shape[1]
    krow = lax.broadcasted_iota(jnp.int32, (k, tm), 0)
    vals = jnp.zeros((k, tm), F32)
    idxs = jnp.zeros((k, tm), jnp.int32)
    for r in range(k):
        vs, ix = list(work), list(ridx)
        while len(vs) > 2:
            later = [vs[a + 1] > vs[a] for a in range(0, len(vs) - 1, 2)]
            odd = len(vs) % 2
            vs = [jnp.where(t, vs[2 * a + 1], vs[2 * a]) for a, t in enumerate(later)] + vs[len(vs) - odd:]
            ix = [jnp.where(t, ix[2 * a + 1], ix[2 * a]) for a, t in enumerate(later)] + ix[len(ix) - odd:]
        v2, i2 = jnp.concatenate(vs, axis=0), jnp.concatenate(ix, axis=0)
        m = jnp.max(v2, axis=0, keepdims=True)
        am = jnp.min(jnp.where(v2 == m, i2, jnp.iinfo(jnp.int32).max), axis=0, keepdims=True)
        vals = jnp.where(krow == r, m, vals)
        idxs = jnp.where(krow == r, am, idxs)
        work = [jnp.where(i == am, -jnp.inf, v) for v, i in zip(work, ridx)]
    return vals, idxs


def _row_tiles(x):
    n, tm = x.shape
    sub = lax.broadcasted_iota(jnp.int32, (SUBLANES, tm), 0)
    return ([x[c:c + SUBLANES, :] for c in range(0, n, SUBLANES)], [sub + c for c in range(0, n, SUBLANES)])


def _candidates(s1, s2, k):
    tm = s1.shape[1]
    sub = lax.broadcasted_iota(jnp.int32, (SUBLANES, tm), 0)
    root = math.isqrt(k)
    vals, idxs = [], []

    def add(v, flat, ok):
        vals.append(jnp.where(ok, v, -jnp.inf))
        idxs.append(jnp.where(ok, flat, k * k))

    for i in range(root):
        nj = k // (i + 1)
        for j0 in range(0, nj, SUBLANES):
            j = sub + j0
            add(s1[i:i + 1, :] + s2[j0:j0 + SUBLANES, :], i * k + j, j < nj)
    for i0 in range(root // SUBLANES * SUBLANES, k, SUBLANES):
        i = sub + i0
        for j in range(k // (root + 1)):
            ni = k // (j + 1)
            if i0 < ni:
                add(s1[i0:i0 + SUBLANES, :] + s2[j:j + 1, :], i * k + j, jnp.logical_and(i >= root, i < ni))
    return vals, idxs


def _select_rows(table, sel, k):
    out = jnp.zeros(sel.shape, table.dtype)
    for i in range(k):
        out = jnp.where(sel == i, table[i:i + 1, :], out)
    return out


def _route_kernel(x_ref, g_ref, wq_ref, sk_ref, h_ref, e_ref, gate_ref):
    x = x_ref[...]
    ms = jnp.mean(x * x, axis=-1, keepdims=True)
    h = x * lax.rsqrt(ms + EPS) * g_ref[...]
    h_ref[...] = h
    q = jnp.dot(h.astype(BF16), wq_ref[...], preferred_element_type=F32)
    k = PEER_TOPK
    experts, gates = [], []
    for hd in range(PEER_HEADS):
        sv, si = [], []
        for c in range(2):
            g = hd * 2 + c
            qg = q[:, g * PEER_HALF:(g + 1) * PEER_HALF].astype(BF16)
            sc = lax.dot_general(sk_ref[g], qg, NT_DIMS, preferred_element_type=F32)
            v, ix = _top_rows(*_row_tiles(sc), k)
            sv.append(v)
            si.append(ix)
        top, idx = _top_rows(*_candidates(sv[0], sv[1], k), k)
        e1 = _select_rows(si[0], lax.shift_right_logical(idx, K_SHIFT), k)
        e2 = _select_rows(si[1], idx & (k - 1), k)
        experts.append(e1 * PEER_KEYS + e2)
        p = jnp.exp(top - top[0:1, :])
        gates.append(p / jnp.sum(p, axis=0, keepdims=True))
    e_ref[...] = jnp.concatenate(experts, axis=0).T
    gate_ref[...] = jnp.concatenate(gates, axis=0).T


def _route(x1, g, wq_bf, sk_bf, *, tm):
    t, d = x1.shape
    nq = wq_bf.shape[1]
    ng, nk, nh = sk_bf.shape
    ek = PEER_HEADS * PEER_TOPK
    return pl.pallas_call(
        _route_kernel,
        grid=(t // tm,),
        in_specs=[pl.BlockSpec((tm, d), lambda i: (i, 0)),
                  pl.BlockSpec((1, d), lambda i: (0, 0)),
                  pl.BlockSpec((d, nq), lambda i: (0, 0)),
                  pl.BlockSpec((ng, nk, nh), lambda i: (0, 0, 0))],
        out_specs=[pl.BlockSpec((tm, d), lambda i: (i, 0)),
                   pl.BlockSpec((tm, ek), lambda i: (i, 0)),
                   pl.BlockSpec((tm, ek), lambda i: (i, 0))],
        out_shape=[jax.ShapeDtypeStruct((t, d), F32),
                   jax.ShapeDtypeStruct((t, ek), jnp.int32),
                   jax.ShapeDtypeStruct((t, ek), F32)],
        compiler_params=_params("parallel"),
        name="peer_route",
    )(x1, g, wq_bf, sk_bf)


def _gelu(x):
    return 0.5 * x * (1.0 + lax.erf(x * (1.0 / math.sqrt(2.0))))


def _peer_kernel(e_ref, g2_ref, h_ref, x_ref, tab_ref, o_ref, buf_ref, sem_ref, *, n_buf):
    tb = o_ref.shape[0]
    ek = e_ref.shape[2]
    half = tab_ref.shape[2] // 2
    step = pl.program_id(0)
    last_step = pl.num_programs(0) - 1
    even = lax.broadcasted_iota(jnp.int32, (1, 2 * ek), 1) % 2 == 0

    cw = half // PEER_CHUNKS
    per = ek // PEER_CHUNKS
    look = n_buf - 3

    def row_copy(t, k, slot):
        return pltpu.make_async_copy(tab_ref.at[e_ref[0, t, k]], buf_ref.at[slot, pl.ds(2 * k, 2)],
                                     sem_ref.at[slot])

    def wait(slot):
        pltpu.make_async_copy(buf_ref.at[slot], buf_ref.at[slot], sem_ref.at[slot]).wait()

    def store(t, o):
        o_ref[pl.ds(t, 1), :half] = x_ref[pl.ds(t, 1), :half] + o[0:1, :]
        o_ref[pl.ds(t, 1), half:] = x_ref[pl.ds(t, 1), half:] + o[1:2, :]

    def s1_lhs(t):
        hrow = h_ref[pl.ds(t, 1), :].astype(BF16)
        return jnp.concatenate([hrow[:, :half], hrow[:, half:]], axis=0)

    def gate_act(r, t):
        s = r[0:1, :] + pltpu.roll(r[1:2, :], 2 * ek - 1, axis=1)
        act = jnp.where(even, _gelu(s) * g2_ref[pl.ds(t, 1), :], 0.0)
        return jnp.concatenate([act, pltpu.roll(act, 1, axis=1)], axis=0).astype(BF16)

    def body(t, slot, carry, *, s1=True, act=True, t_act=None, row_out=None):
        r_prev, a2_prev, o_prev = carry
        if s1:
            if slot % WAIT_GROUP == 0:
                for w in range(WAIT_GROUP):
                    wait(slot + w)
            hh = s1_lhs(t)
        a2 = gate_act(r_prev, t_act) if act else a2_prev
        slot_v, slot_next = (slot - 2) % n_buf, (slot + look) % n_buf
        r = jnp.zeros((2, 2 * ek), F32)
        outs = []
        for c in range(PEER_CHUNKS):
            if s1:
                rows = buf_ref[slot, :, c * cw:(c + 1) * cw]
                r = r + lax.dot_general(hh[:, c * cw:(c + 1) * cw], rows, NT_DIMS,
                                        preferred_element_type=F32)
            rows = buf_ref[slot_v, :, half + c * cw:half + (c + 1) * cw]
            outs.append(jnp.dot(a2_prev, rows, preferred_element_type=F32))
            if s1:
                for k in range(c * per, (c + 1) * per):
                    row_copy(t + look, k, slot_next).start(priority=k % DMA_QUEUES)
        store(row_out, o_prev)
        return r, a2, jnp.concatenate(outs, axis=1)

    @pl.when(step == 0)
    def _():
        buf_ref[n_buf - 2:, :, :] = jnp.zeros((2,) + buf_ref.shape[1:], BF16)
        for j in range(look):
            for k in range(ek):
                row_copy(j, k, j).start(priority=k % DMA_QUEUES)

    def group(g, carry):
        for j in range(n_buf):
            t = g * n_buf + j
            carry = body(t, j, carry, t_act=t - 1 if j >= 1 else jnp.maximum(t - 1, 0),
                         row_out=t - 3 if j >= 3 else jnp.maximum(t - 3, 0))
        return carry

    carry = (jnp.zeros((2, 2 * ek), F32), jnp.zeros((2, 2 * ek), BF16), jnp.zeros((2, half), F32))
    carry = lax.fori_loop(0, tb // n_buf, group, carry)
    carry = body(tb, 0, carry, s1=False, t_act=tb - 1, row_out=tb - 3)
    carry = body(tb + 1, 1, carry, s1=False, act=False, row_out=tb - 2)
    store(tb - 1, carry[2])

    @pl.when(step == last_step)
    def _():
        for j in range(look):
            wait(j)


def _peer(experts, gates2, h, x1, table, *, tb, n_buf):
    t, d = x1.shape
    ek = experts.shape[1]
    n_steps = t // tb
    assert tb % n_buf == 0 and n_buf % WAIT_GROUP == 0 and n_buf - 3 <= SMEM_ROWS
    e3 = experts.reshape(n_steps, tb, ek)
    e_ext = jnp.concatenate([e3, jnp.roll(e3[:, :SMEM_ROWS], -1, axis=0)], axis=1)
    kern = functools.partial(_peer_kernel, n_buf=n_buf)
    return pl.pallas_call(
        kern,
        grid=(n_steps,),
        in_specs=[pl.BlockSpec((1, tb + SMEM_ROWS, ek), lambda i: (i, 0, 0), memory_space=pltpu.SMEM),
                  pl.BlockSpec((tb, 2 * ek), lambda i: (i, 0)),
                  pl.BlockSpec((tb, d), lambda i: (i, 0)),
                  pl.BlockSpec((tb, d), lambda i: (i, 0)),
                  pl.BlockSpec(memory_space=pl.ANY)],
        out_specs=pl.BlockSpec((tb, d), lambda i: (i, 0)),
        out_shape=jax.ShapeDtypeStruct((t, d), F32),
        scratch_shapes=[pltpu.VMEM((n_buf, 2 * ek, table.shape[2]), BF16),
                        pltpu.SemaphoreType.DMA((n_buf,))],
        compiler_params=_params("arbitrary"),
        name="peer_experts",
    )(e_ext, gates2, h, x1, table)


def _pack_kernel(u_ref, v_ref, o_ref):
    half = u_ref.shape[1] // 2
    o_ref[:, 0, :half] = u_ref[:, :half].astype(BF16)
    o_ref[:, 0, half:] = v_ref[:, :half].astype(BF16)
    o_ref[:, 1, :half] = u_ref[:, half:].astype(BF16)
    o_ref[:, 1, half:] = v_ref[:, half:].astype(BF16)


def _pack_table(u, v, *, te):
    e, d = u.shape
    spec = pl.BlockSpec((te, d), lambda i: (i, 0))
    return pl.pallas_call(
        _pack_kernel,
        grid=(e // te,),
        in_specs=[spec, spec],
        out_specs=pl.BlockSpec((te, 2, d), lambda i: (i, 0, 0)),
        out_shape=jax.ShapeDtypeStruct((e, 2, d), BF16),
        compiler_params=_params("parallel"),
        name="peer_pack",
    )(u, v)


def _layer(x2, l, attn_norm_g, w_in, q_norm_g, k_norm_g, lambda_q1, lambda_k1, lambda_q2, lambda_k2,
           subln_g, conv_w, w_out, ffn_norm_g, peer_w_q, peer_subkeys, peer_u, peer_v, *, batch, seq):
    t, d = x2.shape
    attn_w = d // 2
    conv_wd = d - attn_w
    n_heads = attn_w // V_DIM
    lam_init = 0.8 - 0.6 * math.exp(-0.3 * l)
    row = lambda p: p[l].reshape(1, -1).astype(F32)
    row2 = lambda p: jnp.tile(p[l].reshape(1, -1).astype(F32), (1, 2))

    proj = _in_proj(x2, row(attn_norm_g), w_in[l].astype(BF16), tm=512, tn=1536)
    a = _attention(proj, row2(q_norm_g), row2(k_norm_g), row(lambda_q1), row(lambda_k1), row(lambda_q2),
                   row(lambda_k2), row(subln_g), batch=batch, seq=seq, n_heads=n_heads,
                   lam_init=lam_init, tq=256)
    x1 = _mix_out(x2, a, proj, conv_w[l].astype(F32), w_out[l].astype(BF16), seq=seq, tm=256,
                  attn_w=attn_w, conv_wd=conv_wd)
    sk = peer_subkeys[l].reshape(PEER_HEADS * 2, PEER_KEYS, PEER_HALF).astype(BF16)
    h, experts, gates = _route(x1, row(ffn_norm_g), peer_w_q[l].astype(BF16), sk, tm=256)
    gates2 = jnp.stack([gates, jnp.zeros_like(gates)], axis=-1).reshape(t, -1)
    table = _pack_table(peer_u[l], peer_v[l], te=512)
    return _peer(experts, gates2, h, x1, table, tb=128, n_buf=16)


def kernel(x, attn_norm_g, w_in, q_norm_g, k_norm_g, lambda_q1, lambda_k1, lambda_q2, lambda_k2, subln_g,
           conv_w, w_out, ffn_norm_g, peer_w_q, peer_subkeys, peer_u, peer_v):
    b, s, d = x.shape
    x2 = x.reshape(b * s, d)
    for l in range(w_in.shape[0]):
        x2 = _layer(x2, l, attn_norm_g, w_in, q_norm_g, k_norm_g, lambda_q1, lambda_k1, lambda_q2,
                    lambda_k2, subln_g, conv_w, w_out, ffn_norm_g, peer_w_q, peer_subkeys, peer_u, peer_v,
                    batch=b, seq=s)
    return x2.reshape(b, s, d)
```

```python
import functools
import math

import jax
import jax.numpy as jnp
from jax import lax
from jax.experimental import pallas as pl
from jax.experimental.pallas import tpu as pltpu

EPS = 1e-6
HEAD_DIM = 64
V_DIM = 2 * HEAD_DIM
CONV_K = 3
PEER_HEADS = 8
PEER_KEYS = 128
PEER_HALF = 128
PEER_TOPK = 16
K_SHIFT = 4
LANES = 128
SUBLANES = 8
ALIBI_SPLIT = 256
VMEM_LIMIT = 56 * 1024 * 1024

F32 = jnp.float32
BF16 = jnp.bfloat16
NT_DIMS = (((1,), (1,)), ((), ()))


def _params(*sem):
    return pltpu.CompilerParams(dimension_semantics=sem, vmem_limit_bytes=VMEM_LIMIT)


def _in_proj_kernel(x_ref, g_ref, w_ref, o_ref, h_ref):
    @pl.when(pl.program_id(1) == 0)
    def _():
        x = x_ref[...]
        ms = jnp.mean(x * x, axis=-1, keepdims=True)
        h_ref[...] = (x * lax.rsqrt(ms + EPS) * g_ref[...]).astype(BF16)

    o_ref[...] = jnp.dot(h_ref[...], w_ref[...], preferred_element_type=F32).astype(o_ref.dtype)


def _in_proj(x2, g, w_bf, *, tm, tn):
    t, d = x2.shape
    n = w_bf.shape[1]
    return pl.pallas_call(
        _in_proj_kernel,
        grid=(t // tm, n // tn),
        in_specs=[pl.BlockSpec((tm, d), lambda i, j: (i, 0)),
                  pl.BlockSpec((1, d), lambda i, j: (0, 0)),
                  pl.BlockSpec((d, tn), lambda i, j: (0, j))],
        out_specs=pl.BlockSpec((tm, tn), lambda i, j: (i, j)),
        out_shape=jax.ShapeDtypeStruct((t, n), BF16),
        scratch_shapes=[pltpu.VMEM((tm, d), BF16)],
        compiler_params=_params("parallel", "arbitrary"),
        name="in_proj",
    )(x2, g, w_bf)


def _attn_kernel(q_ref, k_ref, v_ref, qg_ref, kg_ref, lq1_ref, lk1_ref, lq2_ref, lk2_ref, sg_ref,
                 o_ref, qa_ref, ka_ref, *, tq, n_heads, lam_init):
    s_len = q_ref.shape[0]
    lane = lax.broadcasted_iota(jnp.int32, (1, LANES), 1)
    first = lane < HEAD_DIM

    lam = (jnp.exp(jnp.sum(lq1_ref[...] * lk1_ref[...], axis=-1, keepdims=True))
           - jnp.exp(jnp.sum(lq2_ref[...] * lk2_ref[...], axis=-1, keepdims=True)) + lam_init)
    head = (pl.program_id(1) + 1).astype(F32)
    slope = jnp.exp2(jnp.full((1, 1), -8.0 / n_heads, F32) * head)

    def qk_norm(x_ref, g_ref):
        x = x_ref[...].astype(F32)
        x2 = x * x
        s1 = jnp.sum(jnp.where(first, x2, 0.0), axis=-1, keepdims=True)
        s2 = jnp.sum(jnp.where(first, 0.0, x2), axis=-1, keepdims=True)
        r = jnp.where(first, lax.rsqrt(s1 / HEAD_DIM + EPS), lax.rsqrt(s2 / HEAD_DIM + EPS))
        return x * r * g_ref[...]

    pos = lax.broadcasted_iota(jnp.int32, (s_len, 1), 0)
    hi = (pos & ~(ALIBI_SPLIT - 1)).astype(F32)
    lo = (pos & (ALIBI_SPLIT - 1)).astype(F32)
    one = jnp.ones((s_len, 1), F32)
    zero = jnp.zeros((s_len, 1), F32)
    q_aug = jnp.where(lane == HEAD_DIM, -slope * hi,
                      jnp.where(lane == HEAD_DIM + 1, -slope * lo,
                                jnp.where(lane <= HEAD_DIM + 3, one, zero)))
    k_aug = jnp.where(lane <= HEAD_DIM + 1, one,
                      jnp.where(lane == HEAD_DIM + 2, slope * hi,
                                jnp.where(lane == HEAD_DIM + 3, slope * lo, zero)))

    qn = qk_norm(q_ref, qg_ref) * (1.0 / math.sqrt(HEAD_DIM))
    kn = qk_norm(k_ref, kg_ref)
    qa_ref[0] = jnp.where(first, qn, q_aug).astype(BF16)
    qa_ref[1] = jnp.where(first, pltpu.roll(qn, HEAD_DIM, axis=1), q_aug).astype(BF16)
    ka_ref[0] = jnp.where(first, kn, k_aug).astype(BF16)
    ka_ref[1] = jnp.where(first, pltpu.roll(kn, HEAD_DIM, axis=1), k_aug).astype(BF16)

    row = lax.broadcasted_iota(jnp.int32, (tq, tq), 0)
    col = lax.broadcasted_iota(jnp.int32, (tq, tq), 1)
    causal = col <= row
    sg = sg_ref[...]

    for i in range(s_len // tq):
        kv = (i + 1) * tq

        def probs(c):
            s = lax.dot_general(qa_ref[c, i * tq:(i + 1) * tq, :], ka_ref[c, :kv, :], NT_DIMS,
                                preferred_element_type=F32)
            diag = jnp.where(causal, s[:, kv - tq:], -jnp.inf)
            s = diag if i == 0 else jnp.concatenate([s[:, :kv - tq], diag], axis=1)
            p = jnp.exp(s - jnp.max(s, axis=-1, keepdims=True))
            return p, jnp.sum(p, axis=-1, keepdims=True)

        p1, l1 = probs(0)
        p2, l2 = probs(1)
        o1 = jnp.dot(p1.astype(BF16), v_ref[:kv, :], preferred_element_type=F32)
        o2 = jnp.dot(p2.astype(BF16), v_ref[:kv, :], preferred_element_type=F32)
        o = o1 * (1.0 / l1) - o2 * (lam / l2)
        ms = jnp.mean(o * o, axis=-1, keepdims=True)
        o = o * lax.rsqrt(ms + EPS) * sg * (1.0 - lam_init)
        o_ref[i * tq:(i + 1) * tq, :] = o.astype(o_ref.dtype)


def _attention(proj, qg2, kg2, lq1, lk1, lq2, lk2, sg, *, batch, seq, n_heads, lam_init, tq):
    t = proj.shape[0]
    small = lambda w: pl.BlockSpec((1, w), lambda b, h: (0, 0))
    kern = functools.partial(_attn_kernel, tq=tq, n_heads=n_heads, lam_init=lam_init)
    return pl.pallas_call(
        kern,
        grid=(batch, n_heads),
        in_specs=[pl.BlockSpec((seq, V_DIM), lambda b, h: (b, h)),
                  pl.BlockSpec((seq, V_DIM), lambda b, h: (b, n_heads + h)),
                  pl.BlockSpec((seq, V_DIM), lambda b, h: (b, 2 * n_heads + h)),
                  small(V_DIM), small(V_DIM), small(HEAD_DIM), small(HEAD_DIM), small(HEAD_DIM),
                  small(HEAD_DIM), small(V_DIM)],
        out_specs=pl.BlockSpec((seq, V_DIM), lambda b, h: (b, h)),
        out_shape=jax.ShapeDtypeStruct((t, n_heads * V_DIM), BF16),
        scratch_shapes=[pltpu.VMEM((2, seq, V_DIM), BF16), pltpu.VMEM((2, seq, V_DIM), BF16)],
        compiler_params=_params("parallel", "parallel"),
        name="diff_attn",
    )(proj, proj, proj, qg2, kg2, lq1, lk1, lq2, lk2, sg)


PEER_CHUNKS = 4
DMA_QUEUES = 2
WAIT_GROUP = 8
SMEM_ROWS = 32
HALO = 16


def _mix_out_kernel(x_ref, a_ref, cb_ref, cc_ref, cx_ref, hc_ref, hx_ref, cw_ref, w_ref, o_ref, u_ref,
                    *, tiles_per_seq):
    tm = x_ref.shape[0]
    i = pl.program_id(0)
    u_ref[HALO:, :] = cc_ref[...].astype(F32) * cx_ref[...].astype(F32)
    halo = hc_ref[...].astype(F32) * hx_ref[...].astype(F32)
    u_ref[:HALO, :] = jnp.where(lax.rem(i, tiles_per_seq) == 0, 0.0, halo)
    cw = cw_ref[...]
    y = u_ref[HALO - 2:HALO - 2 + tm, :] * cw[0:1, :]
    y = y + u_ref[HALO - 1:HALO - 1 + tm, :] * cw[1:2, :]
    y = y + u_ref[HALO:, :] * cw[2:3, :]
    c = (cb_ref[...].astype(F32) * y).astype(BF16)
    aw = a_ref.shape[1]
    o = jnp.dot(a_ref[...], w_ref[:aw, :], preferred_element_type=F32)
    o = o + jnp.dot(c, w_ref[aw:, :], preferred_element_type=F32)
    o_ref[...] = x_ref[...] + o


def _mix_out(x2, a, proj, conv_w, w_out_bf, *, seq, tm, attn_w, conv_wd):
    t, d = x2.shape
    cblk = (3 * attn_w) // conv_wd
    hb = tm // HALO
    halo_map = lambda off: (lambda i: (jnp.maximum(i * hb - 1, 0), cblk + off))
    kern = functools.partial(_mix_out_kernel, tiles_per_seq=seq // tm)
    return pl.pallas_call(
        kern,
        grid=(t // tm,),
        in_specs=[pl.BlockSpec((tm, d), lambda i: (i, 0)),
                  pl.BlockSpec((tm, attn_w), lambda i: (i, 0)),
                  pl.BlockSpec((tm, conv_wd), lambda i: (i, cblk)),
                  pl.BlockSpec((tm, conv_wd), lambda i: (i, cblk + 1)),
                  pl.BlockSpec((tm, conv_wd), lambda i: (i, cblk + 2)),
                  pl.BlockSpec((HALO, conv_wd), halo_map(1)),
                  pl.BlockSpec((HALO, conv_wd), halo_map(2)),
                  pl.BlockSpec((CONV_K, conv_wd), lambda i: (0, 0)),
                  pl.BlockSpec((attn_w + conv_wd, d), lambda i: (0, 0))],
        out_specs=pl.BlockSpec((tm, d), lambda i: (i, 0)),
        out_shape=jax.ShapeDtypeStruct((t, d), F32),
        scratch_shapes=[pltpu.VMEM((tm + HALO, conv_wd), F32)],
        compiler_params=_params("parallel"),
        name="mix_out",
    )(x2, a, proj, proj, proj, proj, proj, conv_w, w_out_bf)


def _top_rows(work, ridx, k):
    tm = work[0].shape[1]
    krow = lax.broadcasted_iota(jnp.int32, (k, tm), 0)
    vals = jnp.zeros((k, tm), F32)
    idxs = jnp.zeros((k, tm), jnp.int32)
    for r in range(k):
        vs, ix = list(work), list(ridx)
        while len(vs) > 2:
            later = [vs[a + 1] > vs[a] for a in range(0, len(vs) - 1, 2)]
            odd = len(vs) % 2
            vs = [jnp.where(t, vs[2 * a + 1], vs[2 * a]) for a, t in enumerate(later)] + vs[len(vs) - odd:]
            ix = [jnp.where(t, ix[2 * a + 1], ix[2 * a]) for a, t in enumerate(later)] + ix[len(ix) - odd:]
        v2, i2 = jnp.concatenate(vs, axis=0), jnp.concatenate(ix, axis=0)
        m = jnp.max(v2, axis=0, keepdims=True)
        am = jnp.min(jnp.where(v2 == m, i2, jnp.iinfo(jnp.int32).max), axis=0, keepdims=True)
        vals = jnp.where(krow == r, m, vals)
        idxs = jnp.where(krow == r, am, idxs)
        work = [jnp.where(i == am, -jnp.inf, v) for v, i in zip(work, ridx)]
    return vals, idxs


def _row_tiles(x):
    n, tm = x.shape
    sub = lax.broadcasted_iota(jnp.int32, (SUBLANES, tm), 0)
    return ([x[c:c + SUBLANES, :] for c in range(0, n, SUBLANES)], [sub + c for c in range(0, n, SUBLANES)])


def _candidates(s1, s2, k):
    tm = s1.shape[1]
    sub = lax.broadcasted_iota(jnp.int32, (SUBLANES, tm), 0)
    root = math.isqrt(k)
    vals, idxs = [], []

    def add(v, flat, ok):
        vals.append(jnp.where(ok, v, -jnp.inf))
        idxs.append(jnp.where(ok, flat, k * k))

    for i in range(root):
        nj = k // (i + 1)
        for j0 in range(0, nj, SUBLANES):
            j = sub + j0
            add(s1[i:i + 1, :] + s2[j0:j0 + SUBLANES, :], i * k + j, j < nj)
    for i0 in range(root // SUBLANES * SUBLANES, k, SUBLANES):
        i = sub + i0
        for j in range(k // (root + 1)):
            ni = k // (j + 1)
            if i0 < ni:
                add(s1[i0:i0 + SUBLANES, :] + s2[j:j + 1, :], i * k + j, jnp.logical_and(i >= root, i < ni))
    return vals, idxs


def _select_rows(table, sel, k):
    out = jnp.zeros(sel.shape, table.dtype)
    for i in range(k):
        out = jnp.where(sel == i, table[i:i + 1, :], out)
    return out


def _route_kernel(x_ref, g_ref, wq_ref, sk_ref, h_ref, e_ref, gate_ref):
    x = x_ref[...]
    ms = jnp.mean(x * x, axis=-1, keepdims=True)
    h = x * lax.rsqrt(ms + EPS) * g_ref[...]
    h_ref[...] = h
    q = jnp.dot(h.astype(BF16), wq_ref[...], preferred_element_type=F32)
    k = PEER_TOPK
    experts, gates = [], []
    for hd in range(PEER_HEADS):
        sv, si = [], []
        for c in range(2):
            g = hd * 2 + c
            qg = q[:, g * PEER_HALF:(g + 1) * PEER_HALF].astype(BF16)
            sc = lax.dot_general(sk_ref[g], qg, NT_DIMS, preferred_element_type=F32)
            v, ix = _top_rows(*_row_tiles(sc), k)
            sv.append(v)
            si.append(ix)
        top, idx = _top_rows(*_candidates(sv[0], sv[1], k), k)
        e1 = _select_rows(si[0], lax.shift_right_logical(idx, K_SHIFT), k)
        e2 = _select_rows(si[1], idx & (k - 1), k)
        experts.append(e1 * PEER_KEYS + e2)
        p = jnp.exp(top - top[0:1, :])
        gates.append(p / jnp.sum(p, axis=0, keepdims=True))
    e_ref[...] = jnp.concatenate(experts, axis=0).T
    gate_ref[...] = jnp.concatenate(gates, axis=0).T


def _route(x1, g, wq_bf, sk_bf, *, tm):
    t, d = x1.shape
    nq = wq_bf.shape[1]
    ng, nk, nh = sk_bf.shape
    ek = PEER_HEADS * PEER_TOPK
    return pl.pallas_call(
        _route_kernel,
        grid=(t // tm,),
        in_specs=[pl.BlockSpec((tm, d), lambda i: (i, 0)),
                  pl.BlockSpec((1, d), lambda i: (0, 0)),
                  pl.BlockSpec((d, nq), lambda i: (0, 0)),
                  pl.BlockSpec((ng, nk, nh), lambda i: (0, 0, 0))],
        out_specs=[pl.BlockSpec((tm, d), lambda i: (i, 0)),
                   pl.BlockSpec((tm, ek), lambda i: (i, 0)),
                   pl.BlockSpec((tm, ek), lambda i: (i, 0))],
        out_shape=[jax.ShapeDtypeStruct((t, d), F32),
                   jax.ShapeDtypeStruct((t, ek), jnp.int32),
                   jax.ShapeDtypeStruct((t, ek), F32)],
        compiler_params=_params("parallel"),
        name="peer_route",
    )(x1, g, wq_bf, sk_bf)


def _gelu(x):
    return 0.5 * x * (1.0 + lax.erf(x * (1.0 / math.sqrt(2.0))))


def _peer_kernel(e_ref, g2_ref, h_ref, x_ref, tab_ref, o_ref, buf_ref, sem_ref, *, n_buf):
    tb = o_ref.shape[0]
    ek = e_ref.shape[2]
    half = tab_ref.shape[2] // 2
    step = pl.program_id(0)
    last_step = pl.num_programs(0) - 1
    even = lax.broadcasted_iota(jnp.int32, (1, 2 * ek), 1) % 2 == 0

    cw = half // PEER_CHUNKS
    per = ek // PEER_CHUNKS
    look = n_buf - 3

    def row_copy(t, k, slot):
        return pltpu.make_async_copy(tab_ref.at[e_ref[0, t, k]], buf_ref.at[slot, pl.ds(2 * k, 2)],
                                     sem_ref.at[slot])

    def wait(slot):
        pltpu.make_async_copy(buf_ref.at[slot], buf_ref.at[slot], sem_ref.at[slot]).wait()

    def store(t, o):
        o_ref[pl.ds(t, 1), :half] = x_ref[pl.ds(t, 1), :half] + o[0:1, :]
        o_ref[pl.ds(t, 1), half:] = x_ref[pl.ds(t, 1), half:] + o[1:2, :]

    def s1_lhs(t):
        hrow = h_ref[pl.ds(t, 1), :].astype(BF16)
        return jnp.concatenate([hrow[:, :half], hrow[:, half:]], axis=0)

    def gate_act(r, t):
        s = r[0:1, :] + pltpu.roll(r[1:2, :], 2 * ek - 1, axis=1)
        act = jnp.where(even, _gelu(s) * g2_ref[pl.ds(t, 1), :], 0.0)
        return jnp.concatenate([act, pltpu.roll(act, 1, axis=1)], axis=0).astype(BF16)

    def body(t, slot, carry, *, s1=True, act=True, t_act=None, row_out=None):
        r_prev, a2_prev, o_prev = carry
        if s1:
            if slot % WAIT_GROUP == 0:
                for w in range(WAIT_GROUP):
                    wait(slot + w)
            hh = s1_lhs(t)
        a2 = gate_act(r_prev, t_act) if act else a2_prev
        slot_v, slot_next = (slot - 2) % n_buf, (slot + look) % n_buf
        r = jnp.zeros((2, 2 * ek), F32)
        outs = []
        for c in range(PEER_CHUNKS):
            if s1:
                rows = buf_ref[slot, :, c * cw:(c + 1) * cw]
                r = r + lax.dot_general(hh[:, c * cw:(c + 1) * cw], rows, NT_DIMS,
                                        preferred_element_type=F32)
            rows = buf_ref[slot_v, :, half + c * cw:half + (c + 1) * cw]
            outs.append(jnp.dot(a2_prev, rows, preferred_element_type=F32))
            if s1:
                for k in range(c * per, (c + 1) * per):
                    row_copy(t + look, k, slot_next).start(priority=k % DMA_QUEUES)
        store(row_out, o_prev)
        return r, a2, jnp.concatenate(outs, axis=1)

    @pl.when(step == 0)
    def _():
        buf_ref[n_buf - 2:, :, :] = jnp.zeros((2,) + buf_ref.shape[1:], BF16)
        for j in range(look):
            for k in range(ek):
                row_copy(j, k, j).start(priority=k % DMA_QUEUES)

    def group(g, carry):
        for j in range(n_buf):
            t = g * n_buf + j
            carry = body(t, j, carry, t_act=t - 1 if j >= 1 else jnp.maximum(t - 1, 0),
                         row_out=t - 3 if j >= 3 else jnp.maximum(t - 3, 0))
        return carry

    carry = (jnp.zeros((2, 2 * ek), F32), jnp.zeros((2, 2 * ek), BF16), jnp.zeros((2, half), F32))
    carry = lax.fori_loop(0, tb // n_buf, group, carry)
    carry = body(tb, 0, carry, s1=False, t_act=tb - 1, row_out=tb - 3)
    carry = body(tb + 1, 1, carry, s1=False, act=False, row_out=tb - 2)
    store(tb - 1, carry[2])

    @pl.when(step == last_step)
    def _():
        for j in range(look):
            wait(j)


def _peer(experts, gates2, h, x1, table, *, tb, n_buf):
    t, d = x1.shape
    ek = experts.shape[1]
    n_steps = t // tb
    assert tb % n_buf == 0 and n_buf % WAIT_GROUP == 0 and n_buf - 3 <= SMEM_ROWS
    e3 = experts.reshape(n_steps, tb, ek)
    e_ext = jnp.concatenate([e3, jnp.roll(e3[:, :SMEM_ROWS], -1, axis=0)], axis=1)
    kern = functools.partial(_peer_kernel, n_buf=n_buf)
    return pl.pallas_call(
        kern,
        grid=(n_steps,),
        in_specs=[pl.BlockSpec((1, tb + SMEM_ROWS, ek), lambda i: (i, 0, 0), memory_space=pltpu.SMEM),
                  pl.BlockSpec((tb, 2 * ek), lambda i: (i, 0)),
                  pl.BlockSpec((tb, d), lambda i: (i, 0)),
                  pl.BlockSpec((tb, d), lambda i: (i, 0)),
                  pl.BlockSpec(memory_space=pl.ANY)],
        out_specs=pl.BlockSpec((tb, d), lambda i: (i, 0)),
        out_shape=jax.ShapeDtypeStruct((t, d), F32),
        scratch_shapes=[pltpu.VMEM((n_buf, 2 * ek, table.shape[2]), BF16),
                        pltpu.SemaphoreType.DMA((n_buf,))],
        compiler_params=_params("arbitrary"),
        name="peer_experts",
    )(e_ext, gates2, h, x1, table)


def _pack_kernel(u_ref, v_ref, o_ref):
    half = u_ref.shape[1] // 2
    o_ref[:, 0, :half] = u_ref[:, :half].astype(BF16)
    o_ref[:, 0, half:] = v_ref[:, :half].astype(BF16)
    o_ref[:, 1, :half] = u_ref[:, half:].astype(BF16)
    o_ref[:, 1, half:] = v_ref[:, half:].astype(BF16)


def _pack_table(u, v, *, te):
    e, d = u.shape
    spec = pl.BlockSpec((te, d), lambda i: (i, 0))
    return pl.pallas_call(
        _pack_kernel,
        grid=(e // te,),
        in_specs=[spec, spec],
        out_specs=pl.BlockSpec((te, 2, d), lambda i: (i, 0, 0)),
        out_shape=jax.ShapeDtypeStruct((e, 2, d), BF16),
        compiler_params=_params("parallel"),
        name="peer_pack",
    )(u, v)


def _layer(x2, l, attn_norm_g, w_in, q_norm_g, k_norm_g, lambda_q1, lambda_k1, lambda_q2, lambda_k2,
           subln_g, conv_w, w_out, ffn_norm_g, peer_w_q, peer_subkeys, peer_u, peer_v, *, batch, seq):
    t, d = x2.shape
    attn_w = d // 2
    conv_wd = d - attn_w
    n_heads = attn_w // V_DIM
    lam_init = 0.8 - 0.6 * math.exp(-0.3 * l)
    row = lambda p: p[l].reshape(1, -1).astype(F32)
    row2 = lambda p: jnp.tile(p[l].reshape(1, -1).astype(F32), (1, 2))

    proj = _in_proj(x2, row(attn_norm_g), w_in[l].astype(BF16), tm=512, tn=1536)
    a = _attention(proj, row2(q_norm_g), row2(k_norm_g), row(lambda_q1), row(lambda_k1), row(lambda_q2),
                   row(lambda_k2), row(subln_g), batch=batch, seq=seq, n_heads=n_heads,
                   lam_init=lam_init, tq=256)
    x1 = _mix_out(x2, a, proj, conv_w[l].astype(F32), w_out[l].astype(BF16), seq=seq, tm=256,
                  attn_w=attn_w, conv_wd=conv_wd)
    sk = peer_subkeys[l].reshape(PEER_HEADS * 2, PEER_KEYS, PEER_HALF).astype(BF16)
    h, experts, gates = _route(x1, row(ffn_norm_g), peer_w_q[l].astype(BF16), sk, tm=256)
    gates2 = jnp.stack([gates, jnp.zeros_like(gates)], axis=-1).reshape(t, -1)
    table = _pack_table(peer_u[l], peer_v[l], te=512)
    return _peer(experts, gates2, h, x1, table, tb=128, n_buf=32)


def kernel(x, attn_norm_g, w_in, q_norm_g, k_norm_g, lambda_q1, lambda_k1, lambda_q2, lambda_k2, subln_g,
           conv_w, w_out, ffn_norm_g, peer_w_q, peer_subkeys, peer_u, peer_v):
    b, s, d = x.shape
    x2 = x.reshape(b * s, d)
    for l in range(w_in.shape[0]):
        x2 = _layer(x2, l, attn_norm_g, w_in, q_norm_g, k_norm_g, lambda_q1, lambda_k1, lambda_q2,
                    lambda_k2, subln_g, conv_w, w_out, ffn_norm_g, peer_w_q, peer_subkeys, peer_u, peer_v,
                    batch=b, seq=s)
    return x2.reshape(b, s, d)
```

```python
import functools
import math

import jax
import jax.numpy as jnp
from jax import lax
from jax.experimental import pallas as pl
from jax.experimental.pallas import tpu as pltpu

EPS = 1e-6
HEAD_DIM = 64
V_DIM = 2 * HEAD_DIM
CONV_K = 3
PEER_HEADS = 8
PEER_KEYS = 128
PEER_HALF = 128
PEER_TOPK = 16
K_SHIFT = 4
LANES = 128
SUBLANES = 8
ALIBI_SPLIT = 256
VMEM_LIMIT = 56 * 1024 * 1024

F32 = jnp.float32
BF16 = jnp.bfloat16
NT_DIMS = (((1,), (1,)), ((), ()))


def _params(*sem):
    return pltpu.CompilerParams(dimension_semantics=sem, vmem_limit_bytes=VMEM_LIMIT)


def _in_proj_kernel(x_ref, g_ref, w_ref, o_ref, h_ref):
    @pl.when(pl.program_id(1) == 0)
    def _():
        x = x_ref[...]
        ms = jnp.mean(x * x, axis=-1, keepdims=True)
        h_ref[...] = (x * lax.rsqrt(ms + EPS) * g_ref[...]).astype(BF16)

    o_ref[...] = jnp.dot(h_ref[...], w_ref[...], preferred_element_type=F32).astype(o_ref.dtype)


def _in_proj(x2, g, w_bf, *, tm, tn):
    t, d = x2.shape
    n = w_bf.shape[1]
    return pl.pallas_call(
        _in_proj_kernel,
        grid=(t // tm, n // tn),
        in_specs=[pl.BlockSpec((tm, d), lambda i, j: (i, 0)),
                  pl.BlockSpec((1, d), lambda i, j: (0, 0)),
                  pl.BlockSpec((d, tn), lambda i, j: (0, j))],
        out_specs=pl.BlockSpec((tm, tn), lambda i, j: (i, j)),
        out_shape=jax.ShapeDtypeStruct((t, n), BF16),
        scratch_shapes=[pltpu.VMEM((tm, d), BF16)],
        compiler_params=_params("parallel", "arbitrary"),
        name="in_proj",
    )(x2, g, w_bf)


def _attn_kernel(q_ref, k_ref, v_ref, qg_ref, kg_ref, lq1_ref, lk1_ref, lq2_ref, lk2_ref, sg_ref,
                 o_ref, qa_ref, ka_ref, *, tq, n_heads, lam_init):
    s_len = q_ref.shape[0]
    lane = lax.broadcasted_iota(jnp.int32, (1, LANES), 1)
    first = lane < HEAD_DIM

    lam = (jnp.exp(jnp.sum(lq1_ref[...] * lk1_ref[...], axis=-1, keepdims=True))
           - jnp.exp(jnp.sum(lq2_ref[...] * lk2_ref[...], axis=-1, keepdims=True)) + lam_init)
    head = (pl.program_id(1) + 1).astype(F32)
    slope = jnp.exp2(jnp.full((1, 1), -8.0 / n_heads, F32) * head)

    def qk_norm(x_ref, g_ref):
        x = x_ref[...].astype(F32)
        x2 = x * x
        s1 = jnp.sum(jnp.where(first, x2, 0.0), axis=-1, keepdims=True)
        s2 = jnp.sum(jnp.where(first, 0.0, x2), axis=-1, keepdims=True)
        r = jnp.where(first, lax.rsqrt(s1 / HEAD_DIM + EPS), lax.rsqrt(s2 / HEAD_DIM + EPS))
        return x * r * g_ref[...]

    pos = lax.broadcasted_iota(jnp.int32, (s_len, 1), 0)
    hi = (pos & ~(ALIBI_SPLIT - 1)).astype(F32)
    lo = (pos & (ALIBI_SPLIT - 1)).astype(F32)
    one = jnp.ones((s_len, 1), F32)
    zero = jnp.zeros((s_len, 1), F32)
    q_aug = jnp.where(lane == HEAD_DIM, -slope * hi,
                      jnp.where(lane == HEAD_DIM + 1, -slope * lo,
                                jnp.where(lane <= HEAD_DIM + 3, one, zero)))
    k_aug = jnp.where(lane <= HEAD_DIM + 1, one,
                      jnp.where(lane == HEAD_DIM + 2, slope * hi,
                                jnp.where(lane == HEAD_DIM + 3, slope * lo, zero)))

    qn = qk_norm(q_ref, qg_ref) * (1.0 / math.sqrt(HEAD_DIM))
    kn = qk_norm(k_ref, kg_ref)
    qa_ref[0] = jnp.where(first, qn, q_aug).astype(BF16)
    qa_ref[1] = jnp.where(first, pltpu.roll(qn, HEAD_DIM, axis=1), q_aug).astype(BF16)
    ka_ref[0] = jnp.where(first, kn, k_aug).astype(BF16)
    ka_ref[1] = jnp.where(first, pltpu.roll(kn, HEAD_DIM, axis=1), k_aug).astype(BF16)

    row = lax.broadcasted_iota(jnp.int32, (tq, tq), 0)
    col = lax.broadcasted_iota(jnp.int32, (tq, tq), 1)
    causal = col <= row
    sg = sg_ref[...]

    for i in range(s_len // tq):
        kv = (i + 1) * tq

        def probs(c):
            s = lax.dot_general(qa_ref[c, i * tq:(i + 1) * tq, :], ka_ref[c, :kv, :], NT_DIMS,
                                preferred_element_type=F32)
            diag = jnp.where(causal, s[:, kv - tq:], -jnp.inf)
            s = diag if i == 0 else jnp.concatenate([s[:, :kv - tq], diag], axis=1)
            p = jnp.exp(s - jnp.max(s, axis=-1, keepdims=True))
            return p, jnp.sum(p, axis=-1, keepdims=True)

        p1, l1 = probs(0)
        p2, l2 = probs(1)
        o1 = jnp.dot(p1.astype(BF16), v_ref[:kv, :], preferred_element_type=F32)
        o2 = jnp.dot(p2.astype(BF16), v_ref[:kv, :], preferred_element_type=F32)
        o = o1 * (1.0 / l1) - o2 * (lam / l2)
        ms = jnp.mean(o * o, axis=-1, keepdims=True)
        o = o * lax.rsqrt(ms + EPS) * sg * (1.0 - lam_init)
        o_ref[i * tq:(i + 1) * tq, :] = o.astype(o_ref.dtype)


def _attention(proj, qg2, kg2, lq1, lk1, lq2, lk2, sg, *, batch, seq, n_heads, lam_init, tq):
    t = proj.shape[0]
    small = lambda w: pl.BlockSpec((1, w), lambda b, h: (0, 0))
    kern = functools.partial(_attn_kernel, tq=tq, n_heads=n_heads, lam_init=lam_init)
    return pl.pallas_call(
        kern,
        grid=(batch, n_heads),
        in_specs=[pl.BlockSpec((seq, V_DIM), lambda b, h: (b, h)),
                  pl.BlockSpec((seq, V_DIM), lambda b, h: (b, n_heads + h)),
                  pl.BlockSpec((seq, V_DIM), lambda b, h: (b, 2 * n_heads + h)),
                  small(V_DIM), small(V_DIM), small(HEAD_DIM), small(HEAD_DIM), small(HEAD_DIM),
                  small(HEAD_DIM), small(V_DIM)],
        out_specs=pl.BlockSpec((seq, V_DIM), lambda b, h: (b, h)),
        out_shape=jax.ShapeDtypeStruct((t, n_heads * V_DIM), BF16),
        scratch_shapes=[pltpu.VMEM((2, seq, V_DIM), BF16), pltpu.VMEM((2, seq, V_DIM), BF16)],
        compiler_params=_params("parallel", "parallel"),
        name="diff_attn",
    )(proj, proj, proj, qg2, kg2, lq1, lk1, lq2, lk2, sg)


PEER_CHUNKS = 4
DMA_QUEUES = 2
WAIT_GROUP = 8
SMEM_ROWS = 16
HALO = 16


def _mix_out_kernel(x_ref, a_ref, cb_ref, cc_ref, cx_ref, hc_ref, hx_ref, cw_ref, w_ref, g_ref, wq_ref, sk_ref,
                    o_ref, h_ref, e_ref, gate_ref, u_ref, *, tiles_per_seq):
    tm = x_ref.shape[0]
    i = pl.program_id(0)
    u_ref[HALO:, :] = cc_ref[...].astype(F32) * cx_ref[...].astype(F32)
    halo = hc_ref[...].astype(F32) * hx_ref[...].astype(F32)
    u_ref[:HALO, :] = jnp.where(lax.rem(i, tiles_per_seq) == 0, 0.0, halo)
    cw = cw_ref[...]
    y = u_ref[HALO - 2:HALO - 2 + tm, :] * cw[0:1, :]
    y = y + u_ref[HALO - 1:HALO - 1 + tm, :] * cw[1:2, :]
    y = y + u_ref[HALO:, :] * cw[2:3, :]
    c = (cb_ref[...].astype(F32) * y).astype(BF16)
    aw = a_ref.shape[1]
    o = jnp.dot(a_ref[...], w_ref[:aw, :], preferred_element_type=F32)
    o = o + jnp.dot(c, w_ref[aw:, :], preferred_element_type=F32)
    x1 = x_ref[...] + o
    o_ref[...] = x1
    _route_body(x1, g_ref, wq_ref, sk_ref, h_ref, e_ref, gate_ref)


def _mix_out(x2, a, proj, conv_w, w_out_bf, g, wq_bf, sk_bf, *, seq, tm, attn_w, conv_wd):
    t, d = x2.shape
    ek = PEER_HEADS * PEER_TOPK
    resident = lambda shape: pl.BlockSpec(shape, lambda i: (0,) * len(shape), pipeline_mode=pl.Buffered(1))
    cblk = (3 * attn_w) // conv_wd
    hb = tm // HALO
    halo_map = lambda off: (lambda i: (jnp.maximum(i * hb - 1, 0), cblk + off))
    kern = functools.partial(_mix_out_kernel, tiles_per_seq=seq // tm)
    return pl.pallas_call(
        kern,
        grid=(t // tm,),
        in_specs=[pl.BlockSpec((tm, d), lambda i: (i, 0)),
                  pl.BlockSpec((tm, attn_w), lambda i: (i, 0)),
                  pl.BlockSpec((tm, conv_wd), lambda i: (i, cblk)),
                  pl.BlockSpec((tm, conv_wd), lambda i: (i, cblk + 1)),
                  pl.BlockSpec((tm, conv_wd), lambda i: (i, cblk + 2)),
                  pl.BlockSpec((HALO, conv_wd), halo_map(1)),
                  pl.BlockSpec((HALO, conv_wd), halo_map(2)),
                  pl.BlockSpec((CONV_K, conv_wd), lambda i: (0, 0)),
                  resident(w_out_bf.shape),
                  pl.BlockSpec((1, d), lambda i: (0, 0)),
                  resident(wq_bf.shape),
                  resident(sk_bf.shape)],
        out_specs=[pl.BlockSpec((tm, d), lambda i: (i, 0)),
                   pl.BlockSpec((tm, d), lambda i: (i, 0)),
                   pl.BlockSpec((tm, ek), lambda i: (i, 0)),
                   pl.BlockSpec((tm, ek), lambda i: (i, 0))],
        out_shape=[jax.ShapeDtypeStruct((t, d), F32),
                   jax.ShapeDtypeStruct((t, d), F32),
                   jax.ShapeDtypeStruct((t, ek), jnp.int32),
                   jax.ShapeDtypeStruct((t, ek), F32)],
        scratch_shapes=[pltpu.VMEM((tm + HALO, conv_wd), F32)],
        compiler_params=_params("parallel"),
        name="mix_route",
    )(x2, a, proj, proj, proj, proj, proj, conv_w, w_out_bf, g, wq_bf, sk_bf)


def _top_rows(work, ridx, k):
    tm = work[0].shape[1]
    krow = lax.broadcasted_iota(jnp.int32, (k, tm), 0)
    vals = jnp.zeros((k, tm), F32)
    idxs = jnp.zeros((k, tm), jnp.int32)
    for r in range(k):
        vs, ix = list(work), list(ridx)
        while len(vs) > 2:
            later = [vs[a + 1] > vs[a] for a in range(0, len(vs) - 1, 2)]
            odd = len(vs) % 2
            vs = [jnp.where(t, vs[2 * a + 1], vs[2 * a]) for a, t in enumerate(later)] + vs[len(vs) - odd:]
            ix = [jnp.where(t, ix[2 * a + 1], ix[2 * a]) for a, t in enumerate(later)] + ix[len(ix) - odd:]
        v2, i2 = jnp.concatenate(vs, axis=0), jnp.concatenate(ix, axis=0)
        m = jnp.max(v2, axis=0, keepdims=True)
        am = jnp.min(jnp.where(v2 == m, i2, jnp.iinfo(jnp.int32).max), axis=0, keepdims=True)
        vals = jnp.where(krow == r, m, vals)
        idxs = jnp.where(krow == r, am, idxs)
        work = [jnp.where(i == am, -jnp.inf, v) for v, i in zip(work, ridx)]
    return vals, idxs


def _row_tiles(x):
    n, tm = x.shape
    sub = lax.broadcasted_iota(jnp.int32, (SUBLANES, tm), 0)
    return ([x[c:c + SUBLANES, :] for c in range(0, n, SUBLANES)], [sub + c for c in range(0, n, SUBLANES)])


def _candidates(s1, s2, k):
    tm = s1.shape[1]
    sub = lax.broadcasted_iota(jnp.int32, (SUBLANES, tm), 0)
    root = math.isqrt(k)
    vals, idxs = [], []

    def add(v, flat, ok):
        vals.append(jnp.where(ok, v, -jnp.inf))
        idxs.append(jnp.where(ok, flat, k * k))

    for i in range(root):
        nj = k // (i + 1)
        for j0 in range(0, nj, SUBLANES):
            j = sub + j0
            add(s1[i:i + 1, :] + s2[j0:j0 + SUBLANES, :], i * k + j, j < nj)
    for i0 in range(root // SUBLANES * SUBLANES, k, SUBLANES):
        i = sub + i0
        for j in range(k // (root + 1)):
            ni = k // (j + 1)
            if i0 < ni:
                add(s1[i0:i0 + SUBLANES, :] + s2[j:j + 1, :], i * k + j, jnp.logical_and(i >= root, i < ni))
    return vals, idxs


def _select_rows(table, sel, k):
    out = jnp.zeros(sel.shape, table.dtype)
    for i in range(k):
        out = jnp.where(sel == i, table[i:i + 1, :], out)
    return out


def _route_body(x, g_ref, wq_ref, sk_ref, h_ref, e_ref, gate_ref):
    ms = jnp.mean(x * x, axis=-1, keepdims=True)
    h = x * lax.rsqrt(ms + EPS) * g_ref[...]
    h_ref[...] = h
    q = jnp.dot(h.astype(BF16), wq_ref[...], preferred_element_type=F32)
    k = PEER_TOPK
    experts, gates = [], []
    for hd in range(PEER_HEADS):
        sv, si = [], []
        for c in range(2):
            g = hd * 2 + c
            qg = q[:, g * PEER_HALF:(g + 1) * PEER_HALF].astype(BF16)
            sc = lax.dot_general(sk_ref[g], qg, NT_DIMS, preferred_element_type=F32)
            v, ix = _top_rows(*_row_tiles(sc), k)
            sv.append(v)
            si.append(ix)
        top, idx = _top_rows(*_candidates(sv[0], sv[1], k), k)
        e1 = _select_rows(si[0], lax.shift_right_logical(idx, K_SHIFT), k)
        e2 = _select_rows(si[1], idx & (k - 1), k)
        experts.append(e1 * PEER_KEYS + e2)
        p = jnp.exp(top - top[0:1, :])
        gates.append(p / jnp.sum(p, axis=0, keepdims=True))
    e_ref[...] = jnp.concatenate(experts, axis=0).T
    gate_ref[...] = jnp.concatenate(gates, axis=0).T


def _gelu(x):
    return 0.5 * x * (1.0 + lax.erf(x * (1.0 / math.sqrt(2.0))))


def _peer_kernel(e_ref, g2_ref, h_ref, x_ref, tab_ref, o_ref, buf_ref, sem_ref, *, n_buf):
    tb = o_ref.shape[0]
    ek = e_ref.shape[2]
    half = tab_ref.shape[2] // 2
    step = pl.program_id(0)
    last_step = pl.num_programs(0) - 1
    even = lax.broadcasted_iota(jnp.int32, (1, 2 * ek), 1) % 2 == 0

    cw = half // PEER_CHUNKS
    per = ek // PEER_CHUNKS
    look = n_buf - 3

    def row_copy(t, k, slot):
        return pltpu.make_async_copy(tab_ref.at[e_ref[0, t, k]], buf_ref.at[slot, pl.ds(2 * k, 2)],
                                     sem_ref.at[slot])

    def wait(slot):
        pltpu.make_async_copy(buf_ref.at[slot], buf_ref.at[slot], sem_ref.at[slot]).wait()

    def store(t, o):
        o_ref[pl.ds(t, 1), :half] = x_ref[pl.ds(t, 1), :half] + o[0:1, :]
        o_ref[pl.ds(t, 1), half:] = x_ref[pl.ds(t, 1), half:] + o[1:2, :]

    def s1_lhs(t):
        hrow = h_ref[pl.ds(t, 1), :].astype(BF16)
        return jnp.concatenate([hrow[:, :half], hrow[:, half:]], axis=0)

    def gate_act(r, t):
        s = r[0:1, :] + pltpu.roll(r[1:2, :], 2 * ek - 1, axis=1)
        act = jnp.where(even, _gelu(s) * g2_ref[pl.ds(t, 1), :], 0.0)
        return jnp.concatenate([act, pltpu.roll(act, 1, axis=1)], axis=0).astype(BF16)

    def body(t, slot, carry, *, s1=True, act=True, t_act=None, row_out=None):
        r_prev, a2_prev, o_prev = carry
        if s1:
            if slot % WAIT_GROUP == 0:
                for w in range(WAIT_GROUP):
                    wait(slot + w)
            hh = s1_lhs(t)
        a2 = gate_act(r_prev, t_act) if act else a2_prev
        slot_v, slot_next = (slot - 2) % n_buf, (slot + look) % n_buf
        r = jnp.zeros((2, 2 * ek), F32)
        outs = []
        for c in range(PEER_CHUNKS):
            if s1:
                rows = buf_ref[slot, :, c * cw:(c + 1) * cw]
                r = r + lax.dot_general(hh[:, c * cw:(c + 1) * cw], rows, NT_DIMS,
                                        preferred_element_type=F32)
            rows = buf_ref[slot_v, :, half + c * cw:half + (c + 1) * cw]
            outs.append(jnp.dot(a2_prev, rows, preferred_element_type=F32))
            if s1:
                for k in range(c * per, (c + 1) * per):
                    row_copy(t + look, k, slot_next).start(priority=k % DMA_QUEUES)
        store(row_out, o_prev)
        return r, a2, jnp.concatenate(outs, axis=1)

    @pl.when(step == 0)
    def _():
        buf_ref[n_buf - 2:, :, :] = jnp.zeros((2,) + buf_ref.shape[1:], BF16)
        for j in range(look):
            for k in range(ek):
                row_copy(j, k, j).start(priority=k % DMA_QUEUES)

    def group(g, carry):
        for j in range(n_buf):
            t = g * n_buf + j
            carry = body(t, j, carry, t_act=t - 1 if j >= 1 else jnp.maximum(t - 1, 0),
                         row_out=t - 3 if j >= 3 else jnp.maximum(t - 3, 0))
        return carry

    carry = (jnp.zeros((2, 2 * ek), F32), jnp.zeros((2, 2 * ek), BF16), jnp.zeros((2, half), F32))
    carry = lax.fori_loop(0, tb // n_buf, group, carry)
    carry = body(tb, 0, carry, s1=False, t_act=tb - 1, row_out=tb - 3)
    carry = body(tb + 1, 1, carry, s1=False, act=False, row_out=tb - 2)
    store(tb - 1, carry[2])

    @pl.when(step == last_step)
    def _():
        for j in range(look):
            wait(j)


def _peer(experts, gates2, h, x1, table, *, tb, n_buf):
    t, d = x1.shape
    ek = experts.shape[1]
    n_steps = t // tb
    assert tb % n_buf == 0 and n_buf % WAIT_GROUP == 0 and n_buf - 3 <= SMEM_ROWS
    e3 = experts.reshape(n_steps, tb, ek)
    e_ext = jnp.concatenate([e3, jnp.roll(e3[:, :SMEM_ROWS], -1, axis=0)], axis=1)
    kern = functools.partial(_peer_kernel, n_buf=n_buf)
    return pl.pallas_call(
        kern,
        grid=(n_steps,),
        in_specs=[pl.BlockSpec((1, tb + SMEM_ROWS, ek), lambda i: (i, 0, 0), memory_space=pltpu.SMEM),
                  pl.BlockSpec((tb, 2 * ek), lambda i: (i, 0)),
                  pl.BlockSpec((tb, d), lambda i: (i, 0)),
                  pl.BlockSpec((tb, d), lambda i: (i, 0)),
                  pl.BlockSpec(memory_space=pl.ANY)],
        out_specs=pl.BlockSpec((tb, d), lambda i: (i, 0)),
        out_shape=jax.ShapeDtypeStruct((t, d), F32),
        scratch_shapes=[pltpu.VMEM((n_buf, 2 * ek, table.shape[2]), BF16),
                        pltpu.SemaphoreType.DMA((n_buf,))],
        compiler_params=_params("arbitrary"),
        name="peer_experts",
    )(e_ext, gates2, h, x1, table)


def _pack_kernel(u_ref, v_ref, o_ref):
    half = u_ref.shape[1] // 2
    o_ref[:, 0, :half] = u_ref[:, :half].astype(BF16)
    o_ref[:, 0, half:] = v_ref[:, :half].astype(BF16)
    o_ref[:, 1, :half] = u_ref[:, half:].astype(BF16)
    o_ref[:, 1, half:] = v_ref[:, half:].astype(BF16)


def _pack_table(u, v, *, te):
    e, d = u.shape
    spec = pl.BlockSpec((te, d), lambda i: (i, 0))
    return pl.pallas_call(
        _pack_kernel,
        grid=(e // te,),
        in_specs=[spec, spec],
        out_specs=pl.BlockSpec((te, 2, d), lambda i: (i, 0, 0)),
        out_shape=jax.ShapeDtypeStruct((e, 2, d), BF16),
        compiler_params=_params("parallel"),
        name="peer_pack",
    )(u, v)


def _layer(x2, l, attn_norm_g, w_in, q_norm_g, k_norm_g, lambda_q1, lambda_k1, lambda_q2, lambda_k2,
           subln_g, conv_w, w_out, ffn_norm_g, peer_w_q, peer_subkeys, peer_u, peer_v, *, batch, seq):
    t, d = x2.shape
    attn_w = d // 2
    conv_wd = d - attn_w
    n_heads = attn_w // V_DIM
    lam_init = 0.8 - 0.6 * math.exp(-0.3 * l)
    row = lambda p: p[l].reshape(1, -1).astype(F32)
    row2 = lambda p: jnp.tile(p[l].reshape(1, -1).astype(F32), (1, 2))

    proj = _in_proj(x2, row(attn_norm_g), w_in[l].astype(BF16), tm=512, tn=1536)
    a = _attention(proj, row2(q_norm_g), row2(k_norm_g), row(lambda_q1), row(lambda_k1), row(lambda_q2),
                   row(lambda_k2), row(subln_g), batch=batch, seq=seq, n_heads=n_heads,
                   lam_init=lam_init, tq=256)
    sk = peer_subkeys[l].reshape(PEER_HEADS * 2, PEER_KEYS, PEER_HALF).astype(BF16)
    x1, h, experts, gates = _mix_out(x2, a, proj, conv_w[l].astype(F32), w_out[l].astype(BF16), row(ffn_norm_g),
                                     peer_w_q[l].astype(BF16), sk, seq=seq, tm=256, attn_w=attn_w, conv_wd=conv_wd)
    gates2 = jnp.stack([gates, jnp.zeros_like(gates)], axis=-1).reshape(t, -1)
    table = _pack_table(peer_u[l], peer_v[l], te=512)
    return _peer(experts, gates2, h, x1, table, tb=128, n_buf=16)


def kernel(x, attn_norm_g, w_in, q_norm_g, k_norm_g, lambda_q1, lambda_k1, lambda_q2, lambda_k2, subln_g,
           conv_w, w_out, ffn_norm_g, peer_w_q, peer_subkeys, peer_u, peer_v):
    b, s, d = x.shape
    x2 = x.reshape(b * s, d)
    for l in range(w_in.shape[0]):
        x2 = _layer(x2, l, attn_norm_g, w_in, q_norm_g, k_norm_g, lambda_q1, lambda_k1, lambda_q2,
                    lambda_k2, subln_g, conv_w, w_out, ffn_norm_g, peer_w_q, peer_subkeys, peer_u, peer_v,
                    batch=b, seq=s)
    return x2.reshape(b, s, d)
```
